```python
import math
import jax, jax.numpy as jnp
from jax import lax
import numpy as np

D_MODEL = 1024
BATCH = 16
SEQ = 256
DEPTH = 2
DEC_BATCH = 4
DEC_SEQ = 4096
PAST_LEN = 256

GRID_W = 64
HEAD_DIM = 64
N_HEADS = 4
ATTN_W = N_HEADS * 2 * HEAD_DIM
LRU_W = D_MODEL // 4
LRU_BLOCKS = 4
LRU_BW = LRU_W // LRU_BLOCKS
CONV_W = 4
FOURIER_W = D_MODEL // 4
FOURIER_GROUPS = 4
FOURIER_GW = FOURIER_W // FOURIER_GROUPS
MIX_W = ATTN_W + LRU_W + FOURIER_W
IN_W = 3 * ATTN_W + 2 * LRU_W + FOURIER_W
FFN_HIDDEN = -(-8 * D_MODEL // (3 * 256)) * 256
ROPE_BASE = 10000.0
RG_C = 8.0
Q_BLOCK = 128
EPS = 1e-6

kernel_name = "hybrid_diff_lru_fourier_prefix_dit_step"


def rmsnorm(x, g):
    xf = x.astype(jnp.float32)
    y = xf * lax.rsqrt(jnp.mean(xf * xf, axis=-1, keepdims=True) + EPS)
    return (y * g.astype(jnp.float32)).astype(x.dtype)


def lambda_init(l):
    return 0.8 - 0.6 * math.exp(-0.3 * l)


def grid_angles(n_tokens):
    rows = n_tokens // GRID_W
    row = jnp.repeat(jnp.arange(rows, dtype=jnp.float32), GRID_W)
    col = jnp.tile(jnp.arange(GRID_W, dtype=jnp.float32), rows)
    n = HEAD_DIM // 4
    inv = ROPE_BASE ** (-jnp.arange(n, dtype=jnp.float32) / n)
    return row[:, None] * inv, col[:, None] * inv


def rotate(x, ang):
    cos = jnp.cos(ang)[None, :, None, None, :].astype(x.dtype)
    sin = jnp.sin(ang)[None, :, None, None, :].astype(x.dtype)
    x1, x2 = jnp.split(x, 2, axis=-1)
    return jnp.concatenate([x1 * cos - x2 * sin, x2 * cos + x1 * sin], axis=-1)


def rope2d(x, ang_r, ang_c):
    h = HEAD_DIM // 2
    return jnp.concatenate([rotate(x[..., :h], ang_r), rotate(x[..., h:], ang_c)], axis=-1)


def diff_attention(q, k, v, lam):
    B, Lq = q.shape[0], q.shape[1]
    nblk = Lq // Q_BLOCK
    qb = jnp.moveaxis(q.reshape(B, nblk, Q_BLOCK, N_HEADS, 2, HEAD_DIM), 1, 0)
    scale = HEAD_DIM ** -0.5

    def block(qi):
        s = jnp.einsum('bqhmd,bkhmd->bhmqk', qi, k).astype(jnp.float32) * scale
        p = jax.nn.softmax(s, axis=-1)
        w = p[:, :, 0] - lam * p[:, :, 1]
        return jnp.einsum('bhqk,bkhe->bqhe', w.astype(v.dtype), v)

    o = lax.map(block, qb)
    return jnp.moveaxis(o, 0, 1).reshape(B, Lq, N_HEADS, 2 * HEAD_DIM)


def dw_conv(x, w, b):
    L = x.shape[1]
    lp = CONV_W // 2
    xp = jnp.pad(x, ((0, 0), (lp, CONV_W - 1 - lp), (0, 0)))
    out = xp[:, 0:L] * w[0]
    for j in range(1, CONV_W):
        out = out + xp[:, j:j + L] * w[j]
    return out + b


def rg_lru_dir(x, wa, ba, wx, bx, lam_p, h0, reverse):
    B, L, _ = x.shape
    xb = x.reshape(B, L, LRU_BLOCKS, LRU_BW)
    r = jax.nn.sigmoid(jnp.einsum('blni,nij->blnj', xb, wa).reshape(B, L, LRU_W) + ba)
    i = jax.nn.sigmoid(jnp.einsum('blni,nij->blnj', xb, wx).reshape(B, L, LRU_W) + bx)
    log_a = -RG_C * r.astype(jnp.float32) * jax.nn.softplus(-lam_p.astype(jnp.float32))
    a = jnp.exp(log_a)
    b = jnp.sqrt(-jnp.expm1(2.0 * log_a)) * (i * x).astype(jnp.float32)
    idx = -1 if reverse else 0
    b = b.at[:, idx].add(a[:, idx] * h0.astype(jnp.float32))

    def comb(e1, e2):
        a1, b1 = e1
        a2, b2 = e2
        return a1 * a2, a2 * b1 + b2

    _, h = lax.associative_scan(comb, (a, b), axis=1, reverse=reverse)
    h_fin = h[:, 0] if reverse else h[:, -1]
    return h.astype(x.dtype), h_fin.astype(x.dtype)


def mixer(P, l, u, ang, ctx_k, ctx_v, h0):
    B, L, _ = u.shape
    proj = u @ P['w_in'][l]
    q, k, v, xr, gr, xf = jnp.split(
        proj, [ATTN_W, 2 * ATTN_W, 3 * ATTN_W, 3 * ATTN_W + LRU_W, 3 * ATTN_W + 2 * LRU_W], axis=-1)
    q = q.reshape(B, L, N_HEADS, 2, HEAD_DIM)
    k = k.reshape(B, L, N_HEADS, 2, HEAD_DIM)
    v = v.reshape(B, L, N_HEADS, 2 * HEAD_DIM)
    if ang is not None:
        q = rope2d(q, ang[0], ang[1])
        k = rope2d(k, ang[0], ang[1])
    if ctx_k is None:
        k_all, v_all = k, v
    else:
        k_all = jnp.concatenate([k, ctx_k.astype(k.dtype)], axis=1)
        v_all = jnp.concatenate([v, ctx_v.astype(v.dtype)], axis=1)
    li = lambda_init(l)
    lp = P['w_lambda'][l].astype(jnp.float32)
    lam = jnp.exp(jnp.sum(lp[0] * lp[1])) - jnp.exp(jnp.sum(lp[2] * lp[3])) + li
    o = diff_attention(q, k_all, v_all, lam)
    attn_out = (rmsnorm(o, P['g_subln'][l]) * (1.0 - li)).reshape(B, L, ATTN_W)
    xc = dw_conv(xr, P['conv_w'][l], P['conv_b'][l])
    hf, hf_fin = rg_lru_dir(xc, P['lru_wa'][l, 0], P['lru_ba'][l, 0], P['lru_wx'][l, 0],
                            P['lru_bx'][l, 0], P['lru_lambda'][l, 0], h0[:, 0], False)
    hb, hb_fin = rg_lru_dir(xc, P['lru_wa'][l, 1], P['lru_ba'][l, 1], P['lru_wx'][l, 1],
                            P['lru_bx'][l, 1], P['lru_lambda'][l, 1], h0[:, 1], True)
    lru_out = (hf + hb) * jax.nn.gelu(gr)
    xg = xf.reshape(B, L, FOURIER_GROUPS, FOURIER_GW).astype(jnp.float32)
    four_out = jnp.fft.fft2(xg, axes=(1, 3), norm='ortho').real.astype(u.dtype).reshape(B, L, FOURIER_W)
    out = jnp.concatenate([attn_out, lru_out, four_out], axis=-1) @ P['w_out'][l]
    return out, k, v, jnp.stack([hf_fin, hb_fin], axis=1)


def layer(P, l, x, cond, ang, ctx_k, ctx_v, h0):
    mod = (jax.nn.silu(cond) @ P['w_mod'][l] + P['b_mod'][l])[:, None, :]
    sh1, sc1, g1, sh2, sc2, g2 = jnp.split(mod, 6, axis=-1)
    h = rmsnorm(x, P['g_pre_mix'][l]) * (1.0 + sc1) + sh1
    o, k, v, hfin = mixer(P, l, h, ang, ctx_k, ctx_v, h0)
    x = x + g1 * rmsnorm(o, P['g_post_mix'][l])
    h = rmsnorm(x, P['g_pre_ffn'][l]) * (1.0 + sc2) + sh2
    gt, up = jnp.split(h @ P['w_gate_up'][l], 2, axis=-1)
    f = (jax.nn.silu(gt) * up) @ P['w_down'][l]
    x = x + g2 * rmsnorm(f, P['g_post_ffn'][l])
    return x, k, v, hfin


def setup_inputs(seed: int = 0) -> dict:
    key = jax.random.key(seed)
    ks = jax.random.split(key, 26)
    f32 = jnp.float32
    D = D_MODEL

    def nrm(k, shape, s=1.0):
        return jax.random.normal(k, shape, f32) * s

    a_c = jax.random.uniform(ks[23], (DEPTH, 2, LRU_W), f32, minval=0.9, maxval=0.999)
    s = a_c ** (1.0 / RG_C)
    lru_lambda = jnp.log(s) - jnp.log1p(-s)
    return {
        'x_prompt': nrm(ks[0], (BATCH, SEQ, D)),
        'x_sample': nrm(ks[1], (DEC_BATCH, DEC_SEQ, D)),
        'cache_k': nrm(ks[2], (DEC_BATCH, DEPTH, PAST_LEN, N_HEADS, 2, HEAD_DIM)),
        'cache_v': nrm(ks[3], (DEC_BATCH, DEPTH, PAST_LEN, N_HEADS, 2 * HEAD_DIM)),
        'state_lru': nrm(ks[4], (DEC_BATCH, DEPTH, 2, LRU_W), 0.5),
        'c': nrm(ks[5], (DEC_BATCH, D)),
        'c_ctx': nrm(ks[6], (D,)),
        'w_mod': nrm(ks[7], (DEPTH, D, 6 * D), 0.5 * D ** -0.5),
        'b_mod': nrm(ks[8], (DEPTH, 6 * D), 0.01),
        'g_pre_mix': 1.0 + nrm(ks[9], (DEPTH, D), 0.05),
        'g_post_mix': 1.0 + nrm(ks[10], (DEPTH, D), 0.05),
        'g_pre_ffn': 1.0 + nrm(ks[11], (DEPTH, D), 0.05),
        'g_post_ffn': 1.0 + nrm(ks[12], (DEPTH, D), 0.05),
        'w_in': nrm(ks[13], (DEPTH, D, IN_W), D ** -0.5),
        'w_out': nrm(ks[14], (DEPTH, MIX_W, D), MIX_W ** -0.5),
        'w_lambda': nrm(ks[15], (DEPTH, 4, HEAD_DIM), 0.1),
        'g_subln': 1.0 + nrm(ks[16], (DEPTH, 2 * HEAD_DIM), 0.05),
        'conv_w': nrm(ks[17], (DEPTH, CONV_W, LRU_W), CONV_W ** -0.5),
        'conv_b': nrm(ks[18], (DEPTH, LRU_W), 0.01),
        'lru_wa': nrm(ks[19], (DEPTH, 2, LRU_BLOCKS, LRU_BW, LRU_BW), LRU_BW ** -0.5),
        'lru_ba': nrm(ks[20], (DEPTH, 2, LRU_W), 0.01),
        'lru_wx': nrm(ks[21], (DEPTH, 2, LRU_BLOCKS, LRU_BW, LRU_BW), LRU_BW ** -0.5),
        'lru_bx': nrm(ks[22], (DEPTH, 2, LRU_W), 0.01),
        'lru_lambda': lru_lambda,
        'w_gate_up': nrm(ks[24], (DEPTH, D, 2 * FFN_HIDDEN), D ** -0.5),
        'w_down': nrm(ks[25], (DEPTH, FFN_HIDDEN, D), FFN_HIDDEN ** -0.5),
    }


def reference(x_prompt, x_sample, cache_k, cache_v, state_lru, c, c_ctx,
              w_mod, b_mod, g_pre_mix, g_post_mix, g_pre_ffn, g_post_ffn,
              w_in, w_out, w_lambda, g_subln, conv_w, conv_b,
              lru_wa, lru_ba, lru_wx, lru_bx, lru_lambda, w_gate_up, w_down):
    P = {'w_mod': w_mod, 'b_mod': b_mod, 'g_pre_mix': g_pre_mix, 'g_post_mix': g_post_mix,
         'g_pre_ffn': g_pre_ffn, 'g_post_ffn': g_post_ffn, 'w_in': w_in, 'w_out': w_out,
         'w_lambda': w_lambda, 'g_subln': g_subln, 'conv_w': conv_w, 'conv_b': conv_b,
         'lru_wa': lru_wa, 'lru_ba': lru_ba, 'lru_wx': lru_wx, 'lru_bx': lru_bx,
         'lru_lambda': lru_lambda, 'w_gate_up': w_gate_up, 'w_down': w_down}
    ang = grid_angles(x_sample.shape[1])
    xp, xs = x_prompt, x_sample
    h0_ctx = jnp.zeros((xp.shape[0], 2, LRU_W), xp.dtype)
    ks_, vs_, hs_ = [], [], []
    for l in range(DEPTH):
        xp, k_l, v_l, h_l = layer(P, l, xp, c_ctx[None, :], None, None, None, h0_ctx)
        ks_.append(k_l)
        vs_.append(v_l)
        hs_.append(h_l)
        xs, _, _, _ = layer(P, l, xs, c, ang, cache_k[:, l], cache_v[:, l], state_lru[:, l])
    new_k = jnp.stack(ks_, axis=1)
    new_v = jnp.stack(vs_, axis=1)
    new_h = jnp.stack(hs_, axis=1)
    return (xp, xs, new_k, new_v, new_h)
```

```python
import functools
import math

import jax
import jax.numpy as jnp
import numpy as np
from jax import lax
from jax.experimental import pallas as pl
from jax.experimental.pallas import tpu as pltpu

D_MODEL = 1024
DEPTH = 2
GRID_W = 64
HEAD_DIM = 64
N_HEADS = 4
HEAD_W = 2 * HEAD_DIM
ATTN_W = N_HEADS * HEAD_W
LRU_W = D_MODEL // 4
LRU_BLOCKS = 4
CONV_W = 4
FOURIER_W = D_MODEL // 4
FOURIER_GROUPS = 4
FOURIER_GW = FOURIER_W // FOURIER_GROUPS
MIX_W = ATTN_W + LRU_W + FOURIER_W
REST_W = 2 * LRU_W + FOURIER_W
IN_W = 3 * ATTN_W + REST_W
FFN_HIDDEN = -(-8 * D_MODEL // (3 * 256)) * 256
ROPE_BASE = 10000.0
RG_C = 8.0
EPS = 1e-6
N_MOD = 6
COND_ROWS = 8

F32 = jnp.float32
BF16 = jnp.bfloat16
V7X_VMEM_LIMIT_BYTES = 56 * 1024 * 1024
CONV_HALO = 8


def _cparams(*semantics):
    return pltpu.CompilerParams(dimension_semantics=semantics,
                                vmem_limit_bytes=V7X_VMEM_LIMIT_BYTES)


def _resident(shape, index_map):
    return pl.BlockSpec(shape, index_map, pipeline_mode=pl.Buffered(1))


def _lambda_init(l):
    return 0.8 - 0.6 * math.exp(-0.3 * l)


def _rms(x, axis):
    return x * lax.rsqrt(jnp.mean(x * x, axis=axis, keepdims=True) + EPS)


def _mod_kernel(cond_ref, w_ref, b_ref, o_ref):
    s = cond_ref[...]
    s = s * jax.nn.sigmoid(s)
    o_ref[0] = jnp.dot(s, w_ref[0], precision=lax.Precision.HIGHEST,
                       preferred_element_type=F32) + b_ref[0]


def _modulation(cond, w_mod, b_mod):
    tn = 1536
    n_out = N_MOD * D_MODEL
    return pl.pallas_call(
        _mod_kernel,
        grid=(DEPTH, n_out // tn),
        in_specs=[pl.BlockSpec((COND_ROWS, D_MODEL), lambda l, j: (0, 0)),
                  pl.BlockSpec((1, D_MODEL, tn), lambda l, j: (l, 0, j)),
                  pl.BlockSpec((1, 1, tn), lambda l, j: (l, 0, j))],
        out_specs=pl.BlockSpec((1, COND_ROWS, tn), lambda l, j: (l, 0, j)),
        out_shape=jax.ShapeDtypeStruct((DEPTH, COND_ROWS, n_out), F32),
        compiler_params=_cparams("arbitrary", "arbitrary"),
        name="modulation",
    )(cond, w_mod, b_mod.reshape(DEPTH, 1, n_out))


def _inproj_kernel(*refs, rope, emit_kv, tm):
    x_ref, mod_ref, g_ref, w_ref = refs[:4]
    refs = refs[4:]
    if rope:
        cos_ref, sin_lo_ref, sin_hi_ref = refs[:3]
        refs = refs[3:]
    qt_ref, k_ref, vt_ref, rest_ref = refs[:4]
    x = x_ref[0]
    mod = mod_ref[0]
    h = (_rms(x, -1) * g_ref[...]) * (1.0 + mod[1:2]) + mod[0:1]
    proj = jnp.dot(h.astype(BF16), w_ref[...], preferred_element_type=F32)
    q = proj[:, :ATTN_W]
    k = proj[:, ATTN_W:2 * ATTN_W]
    v = proj[:, 2 * ATTN_W:3 * ATTN_W]
    if emit_kv:
        kf_ref, vf_ref = refs[4:6]
        kf_ref[...] = k
        vf_ref[...] = v

    def rotary(t):
        return (t * cos_ref[...] + pltpu.roll(t, HEAD_W - 16, 1) * sin_lo_ref[...]
                + pltpu.roll(t, 16, 1) * sin_hi_ref[...])

    row = lax.broadcasted_iota(jnp.int32, (HEAD_W, tm), 0)
    scale = HEAD_DIM ** -0.5
    for hd in range(N_HEADS):
        qh = q[:, hd * HEAD_W:(hd + 1) * HEAD_W]
        kh = k[:, hd * HEAD_W:(hd + 1) * HEAD_W]
        if rope:
            qh = rotary(qh)
            kh = rotary(kh)
        qht = (qh * scale).T
        qt_ref[0, 2 * hd] = jnp.where(row < HEAD_DIM, qht, 0.0).astype(BF16)
        qt_ref[0, 2 * hd + 1] = jnp.where(row >= HEAD_DIM, qht, 0.0).astype(BF16)
        k_ref[0, :, hd * HEAD_W:(hd + 1) * HEAD_W] = kh.astype(BF16)
    vt_ref[0] = v.T.astype(BF16)
    rest_ref[0] = proj[:, 3 * ATTN_W:]


def _inproj(x, mod, goff, g_pre, w_in, rope_tabs, emit_kv):
    G, T, _ = x.shape
    tm = 512
    rope = rope_tabs is not None
    in_specs = [pl.BlockSpec((1, tm, D_MODEL), lambda g, i: (g, i, 0)),
                pl.BlockSpec((1, N_MOD, D_MODEL), lambda g, i: (g + goff, 0, 0)),
                pl.BlockSpec((1, D_MODEL), lambda g, i: (0, 0)),
                _resident((D_MODEL, IN_W), lambda g, i: (0, 0))]
    args = [x, mod, g_pre, w_in]
    if rope:
        in_specs += [pl.BlockSpec((tm, HEAD_W), lambda g, i: (i, 0))] * 3
        args += list(rope_tabs)
    out_specs = [pl.BlockSpec((1, 2 * N_HEADS, HEAD_W, tm), lambda g, i: (g, 0, 0, i)),
                 pl.BlockSpec((1, tm, ATTN_W), lambda g, i: (g, i, 0)),
                 pl.BlockSpec((1, ATTN_W, tm), lambda g, i: (g, 0, i)),
                 pl.BlockSpec((1, tm, REST_W), lambda g, i: (g, i, 0))]
    out_shape = [jax.ShapeDtypeStruct((G, 2 * N_HEADS, HEAD_W, T), BF16),
                 jax.ShapeDtypeStruct((G, T, ATTN_W), BF16),
                 jax.ShapeDtypeStruct((G, ATTN_W, T), BF16),
                 jax.ShapeDtypeStruct((G, T, REST_W), F32)]
    if emit_kv:
        assert G == 1
        out_specs += [pl.BlockSpec((tm, ATTN_W), lambda g, i: (i, 0))] * 2
        out_shape += [jax.ShapeDtypeStruct((T, ATTN_W), F32)] * 2
    return pl.pallas_call(
        functools.partial(_inproj_kernel, rope=rope, emit_kv=emit_kv, tm=tm),
        grid=(G, T // tm),
        in_specs=in_specs, out_specs=out_specs, out_shape=out_shape,
        compiler_params=_cparams("arbitrary", "arbitrary"),
        name="inproj_rope" if rope else "inproj_ctx",
    )(*args)


def _attn_kernel(*refs, has_cache, li):
    qt_ref, k_ref, vt_ref = refs[:3]
    refs = refs[3:]
    if has_cache:
        ck_ref, cv_ref = refs[:2]
        refs = refs[2:]
    wl_ref, g_ref, o_ref = refs
    k = k_ref[0]
    vt = vt_ref[0]
    if has_cache:
        ck = ck_ref[0].astype(BF16)
        cv = cv_ref[0].astype(BF16)
    maps = []
    for m in range(2):
        qt = qt_ref[0, m]
        s = jnp.dot(k, qt, preferred_element_type=F32)
        mx = jnp.max(s, axis=0, keepdims=True)
        if has_cache:
            sc = jnp.dot(ck, qt, preferred_element_type=F32)
            mx = jnp.maximum(mx, jnp.max(sc, axis=0, keepdims=True))
        e = jnp.exp(s - mx)
        den = jnp.sum(e, axis=0, keepdims=True)
        acc = jnp.dot(vt, e.astype(BF16), preferred_element_type=F32)
        if has_cache:
            ec = jnp.exp(sc - mx)
            den = den + jnp.sum(ec, axis=0, keepdims=True)
            acc = acc + lax.dot_general(cv, ec.astype(BF16), (((0,), (0,)), ((), ())),
                                        preferred_element_type=F32)
        maps.append(acc / den)
    wl = wl_ref[...]
    lam = (jnp.exp(jnp.sum(wl[0:1] * wl[1:2], axis=-1, keepdims=True))
           - jnp.exp(jnp.sum(wl[2:3] * wl[3:4], axis=-1, keepdims=True)) + li)
    ot = maps[0] - lam * maps[1]
    y = (_rms(ot, 0) * g_ref[...]) * (1.0 - li)
    o_ref[0] = y.T.astype(BF16)


def _attention(qt, k, vt, cache, w_lambda_l, g_subln_l, li, n_seq, seq_len, tq):
    G, _, _, T = qt.shape
    per_g = T // seq_len
    nq = seq_len // tq
    has_cache = cache is not None

    def gi(b):
        return b // per_g

    def si(b):
        return b % per_g

    in_specs = [pl.BlockSpec((1, 2, HEAD_W, tq), lambda b, h, i: (gi(b), h, 0, si(b) * nq + i)),
                pl.BlockSpec((1, seq_len, HEAD_W), lambda b, h, i: (gi(b), si(b), h)),
                pl.BlockSpec((1, HEAD_W, seq_len), lambda b, h, i: (gi(b), h, si(b)))]
    args = [qt, k, vt]
    if has_cache:
        ck, cv, layer = cache
        past = ck.shape[2]
        in_specs += [pl.BlockSpec((1, None, past, HEAD_W), lambda b, h, i: (b, layer, 0, h))] * 2
        args += [ck, cv]
    in_specs += [pl.BlockSpec((4, HEAD_DIM), lambda b, h, i: (0, 0)),
                 pl.BlockSpec((HEAD_W, 1), lambda b, h, i: (0, 0))]
    args += [w_lambda_l, g_subln_l.reshape(HEAD_W, 1)]
    return pl.pallas_call(
        functools.partial(_attn_kernel, has_cache=has_cache, li=li),
        grid=(n_seq, N_HEADS, nq),
        in_specs=in_specs,
        out_specs=pl.BlockSpec((1, tq, HEAD_W), lambda b, h, i: (gi(b), si(b) * nq + i, h)),
        out_shape=jax.ShapeDtypeStruct((G, T, ATTN_W), BF16),
        compiler_params=_cparams("arbitrary", "arbitrary", "arbitrary"),
        name="attn_latent" if has_cache else "attn_ctx",
    )(*args)


def _scan_chunk(a, b, row, reverse):
    n = a.shape[0]
    d = 1
    while d < n:
        if reverse:
            shift, valid = n - d, row < n - d
        else:
            shift, valid = d, row >= d
        a_prev = jnp.where(valid, pltpu.roll(a, shift, 0), 1.0)
        b_prev = jnp.where(valid, pltpu.roll(b, shift, 0), 0.0)
        b = a * b_prev + b
        a = a * a_prev
        d *= 2
    return a, b


def _lru_kernel(xr_ref, gr_ref, cw_ref, cb_ref, w_ref, bias_ref, lam_ref, h0_ref,
                out_ref, hfin_ref, xpad, hf_scr, ab_scr, bb_scr, *, seq_len, tc):
    nch = seq_len // tc
    zeros_halo = jnp.zeros((CONV_HALO, LRU_W), F32)
    xpad[0:CONV_HALO] = zeros_halo
    xpad[seq_len + CONV_HALO:seq_len + 2 * CONV_HALO] = zeros_halo
    xpad[CONV_HALO:seq_len + CONV_HALO] = xr_ref[0]
    lam_p = lam_ref[...]
    neg = -lam_p
    softplus = jnp.maximum(neg, 0.0) + jnp.log1p(jnp.exp(-jnp.abs(neg)))
    coef = -RG_C * softplus
    cw = cw_ref[...]
    row = lax.broadcasted_iota(jnp.int32, (tc, LRU_W), 0)
    h0 = h0_ref[0]

    def decay_and_input(xc, z, direction):
        off = direction * 2 * LRU_W
        r = jax.nn.sigmoid(z[:, off:off + LRU_W])
        i = jax.nn.sigmoid(z[:, off + LRU_W:off + 2 * LRU_W])
        log_a = coef[direction:direction + 1] * r
        a = jnp.exp(log_a)
        b = jnp.sqrt(-jnp.tanh(log_a) * (1.0 + a * a)) * (i * xc)
        return a, b

    def forward_chunk(c, carry):
        start = pl.multiple_of(c * tc, tc)
        win = xpad[pl.ds(start, tc + 2 * CONV_HALO), :]
        lo = CONV_HALO - CONV_W // 2
        xc = win[lo:lo + tc] * cw[0:1]
        for j in range(1, CONV_W):
            xc = xc + win[lo + j:lo + j + tc] * cw[j:j + 1]
        xc = xc + cb_ref[...]
        z = jnp.dot(xc.astype(BF16), w_ref[...], preferred_element_type=F32) + bias_ref[...]
        a_f, b_f = decay_and_input(xc, z, 0)
        a_cum, h_loc = _scan_chunk(a_f, b_f, row, reverse=False)
        hf = a_cum * carry + h_loc
        hf_scr[pl.ds(start, tc), :] = hf
        a_b, b_b = decay_and_input(xc, z, 1)
        ab_scr[pl.ds(start, tc), :] = a_b
        bb_scr[pl.ds(start, tc), :] = b_b
        return hf[tc - 1:tc]

    def backward_chunk(j, carry):
        c = nch - 1 - j
        start = pl.multiple_of(c * tc, tc)
        a_cum, h_loc = _scan_chunk(ab_scr[pl.ds(start, tc), :], bb_scr[pl.ds(start, tc), :],
                                   row, reverse=True)
        hb = a_cum * carry + h_loc
        gate = jax.nn.gelu(gr_ref[0, pl.ds(start, tc), :], approximate=True)
        out_ref[0, pl.ds(start, tc), :] = ((hf_scr[pl.ds(start, tc), :] + hb) * gate).astype(BF16)
        return hb[0:1]

    if nch == 1:
        hf_fin = forward_chunk(0, h0[0:1])
        hb_fin = backward_chunk(0, h0[1:2])
    else:
        hf_fin = lax.fori_loop(0, nch, forward_chunk, h0[0:1])
        hb_fin = lax.fori_loop(0, nch, backward_chunk, h0[1:2])
    hfin_ref[0, 0:1] = hf_fin
    hfin_ref[0, 1:2] = hb_fin


def _rg_lru(rest, conv_w, conv_b, w_gates, b_gates, lam_p, h0):
    B, L, _ = rest.shape
    tc = min(L, 256)
    return pl.pallas_call(
        functools.partial(_lru_kernel, seq_len=L, tc=tc),
        grid=(B,),
        in_specs=[pl.BlockSpec((1, L, LRU_W), lambda b: (b, 0, 0)),
                  pl.BlockSpec((1, L, LRU_W), lambda b: (b, 0, 1)),
                  pl.BlockSpec((CONV_W, LRU_W), lambda b: (0, 0)),
                  pl.BlockSpec((1, LRU_W), lambda b: (0, 0)),
                  pl.BlockSpec((LRU_W, 4 * LRU_W), lambda b: (0, 0)),
                  pl.BlockSpec((1, 4 * LRU_W), lambda b: (0, 0)),
                  pl.BlockSpec((2, LRU_W), lambda b: (0, 0)),
                  pl.BlockSpec((1, 2, LRU_W), lambda b: (b, 0, 0))],
        out_specs=[pl.BlockSpec((1, L, LRU_W), lambda b: (b, 0, 0)),
                   pl.BlockSpec((1, 2, LRU_W), lambda b: (b, 0, 0))],
        out_shape=[jax.ShapeDtypeStruct((B, L, LRU_W), BF16),
                   jax.ShapeDtypeStruct((B, 2, LRU_W), F32)],
        scratch_shapes=[pltpu.VMEM((L + 2 * CONV_HALO, LRU_W), F32),
                        pltpu.VMEM((L, LRU_W), F32),
                        pltpu.VMEM((L, LRU_W), F32),
                        pltpu.VMEM((L, LRU_W), F32)],
        compiler_params=_cparams("arbitrary"),
        name="rg_lru",
    )(rest, rest, conv_w, conv_b.reshape(1, LRU_W), w_gates, b_gates, lam_p, h0)


def _chan_dft_kernel(x_ref, w_ref, yc_ref, ys_ref):
    y = jnp.dot(x_ref[0].astype(BF16), w_ref[...], preferred_element_type=F32)
    yc_ref[...] = y[:, :FOURIER_W].astype(BF16)
    ys_ref[...] = y[:, FOURIER_W:].astype(BF16)


def _seq_dft_kernel(fc_ref, fs_ref, yc_ref, ys_ref, o_ref, *, scale):
    acc = jnp.dot(fc_ref[...], yc_ref[...], preferred_element_type=F32)
    acc = acc - jnp.dot(fs_ref[...], ys_ref[...], preferred_element_type=F32)
    o_ref[...] = (acc * scale).astype(BF16)


def _fourier(rest, chan_dft, seq_cos, seq_sin):
    B, L, _ = rest.shape
    tm = min(L, 512)
    xf_block = (2 * LRU_W) // FOURIER_W
    y_shape = jax.ShapeDtypeStruct((L, B * FOURIER_W), BF16)
    yc, ys = pl.pallas_call(
        _chan_dft_kernel,
        grid=(B, L // tm),
        in_specs=[pl.BlockSpec((1, tm, FOURIER_W), lambda b, i: (b, i, xf_block)),
                  pl.BlockSpec((FOURIER_W, 2 * FOURIER_W), lambda b, i: (0, 0))],
        out_specs=[pl.BlockSpec((tm, FOURIER_W), lambda b, i: (i, b))] * 2,
        out_shape=[y_shape, y_shape],
        compiler_params=_cparams("arbitrary", "arbitrary"),
        name="fourier_channels",
    )(rest, chan_dft)
    tr = min(L, 256)
    return pl.pallas_call(
        functools.partial(_seq_dft_kernel, scale=(L * FOURIER_GW) ** -0.5),
        grid=(L // tr,),
        in_specs=[pl.BlockSpec((tr, L), lambda i: (i, 0)),
                  pl.BlockSpec((tr, L), lambda i: (i, 0)),
                  _resident((L, B * FOURIER_W), lambda i: (0, 0)),
                  _resident((L, B * FOURIER_W), lambda i: (0, 0))],
        out_specs=pl.BlockSpec((tr, B * FOURIER_W), lambda i: (i, 0)),
        out_shape=y_shape,
        compiler_params=_cparams("arbitrary"),
        name="fourier_sequence",
    )(seq_cos, seq_sin, yc, ys)


def _post_kernel(x_ref, attn_ref, lru_ref, four_ref, mod_ref, gmix_ref, gpre_ref, gffn_ref,
                 wo_ref, wgu_ref, wd_ref, o_ref):
    mod = mod_ref[0]
    o = jnp.dot(attn_ref[0], wo_ref[0:ATTN_W], preferred_element_type=F32)
    o = o + jnp.dot(lru_ref[0], wo_ref[ATTN_W:ATTN_W + LRU_W], preferred_element_type=F32)
    o = o + jnp.dot(four_ref[...], wo_ref[ATTN_W + LRU_W:MIX_W], preferred_element_type=F32)
    x1 = x_ref[0] + mod[2:3] * (_rms(o, -1) * gmix_ref[...])
    h = (_rms(x1, -1) * gpre_ref[...]) * (1.0 + mod[4:5]) + mod[3:4]
    hb = h.astype(BF16)
    gt = jnp.dot(hb, wgu_ref[:, :FFN_HIDDEN], preferred_element_type=F32)
    up = jnp.dot(hb, wgu_ref[:, FFN_HIDDEN:], preferred_element_type=F32)
    act = (gt * jax.nn.sigmoid(gt)) * up
    f = jnp.dot(act.astype(BF16), wd_ref[...], preferred_element_type=F32)
    o_ref[0] = x1 + mod[5:6] * (_rms(f, -1) * gffn_ref[...])


def _post(x, attn, lru, four, four_spec, mod, goff, g_post_mix, g_pre_ffn, g_post_ffn,
          w_out, w_gate_up, w_down, tm):
    G, T, _ = x.shape
    vec = pl.BlockSpec((1, D_MODEL), lambda g, i: (0, 0))
    return pl.pallas_call(
        _post_kernel,
        grid=(G, T // tm),
        in_specs=[pl.BlockSpec((1, tm, D_MODEL), lambda g, i: (g, i, 0)),
                  pl.BlockSpec((1, tm, ATTN_W), lambda g, i: (g, i, 0)),
                  pl.BlockSpec((1, tm, LRU_W), lambda g, i: (g, i, 0)),
                  four_spec,
                  pl.BlockSpec((1, N_MOD, D_MODEL), lambda g, i: (g + goff, 0, 0)),
                  vec, vec, vec,
                  _resident((MIX_W, D_MODEL), lambda g, i: (0, 0)),
                  _resident((D_MODEL, 2 * FFN_HIDDEN), lambda g, i: (0, 0)),
                  _resident((FFN_HIDDEN, D_MODEL), lambda g, i: (0, 0))],
        out_specs=pl.BlockSpec((1, tm, D_MODEL), lambda g, i: (g, i, 0)),
        out_shape=jax.ShapeDtypeStruct((G, T, D_MODEL), F32),
        compiler_params=_cparams("arbitrary", "arbitrary"),
        name="post",
    )(x, attn, lru, four, mod, g_post_mix, g_pre_ffn, g_post_ffn, w_out, w_gate_up, w_down)


def _rope_tables(n_tokens):
    rows = n_tokens // GRID_W
    row = jnp.repeat(jnp.arange(rows, dtype=F32), GRID_W)
    col = jnp.tile(jnp.arange(GRID_W, dtype=F32), rows)
    n = HEAD_DIM // 4
    inv = ROPE_BASE ** (-jnp.arange(n, dtype=F32) / n)
    ang_r, ang_c = row[:, None] * inv, col[:, None] * inv
    zero = jnp.zeros_like(ang_r)
    cos64 = jnp.concatenate([jnp.cos(ang_r)] * 2 + [jnp.cos(ang_c)] * 2, axis=-1)
    sin_lo = jnp.concatenate([-jnp.sin(ang_r), zero, -jnp.sin(ang_c), zero], axis=-1)
    sin_hi = jnp.concatenate([zero, jnp.sin(ang_r), zero, jnp.sin(ang_c)], axis=-1)
    return tuple(jnp.tile(t, (1, 2)) for t in (cos64, sin_lo, sin_hi))


def _seq_dft_tables(n):
    k = jnp.arange(n, dtype=jnp.int32)
    ang = ((k[:, None] * k[None, :]) % n).astype(F32) * (2.0 * math.pi / n)
    return jnp.cos(ang).astype(BF16), jnp.sin(ang).astype(BF16)


def _chan_dft_table():
    k = np.arange(FOURIER_GW)
    ang = 2.0 * np.pi * ((k[:, None] * k[None, :]) % FOURIER_GW) / FOURIER_GW
    eye = np.eye(FOURIER_GROUPS)
    table = np.concatenate([np.kron(eye, np.cos(ang)), np.kron(eye, np.sin(ang))], axis=1)
    return jnp.asarray(table, dtype=F32).astype(BF16)


def _block_diag(w):
    n, bw, _ = w.shape
    eye = jnp.eye(n, dtype=w.dtype)
    return (eye[:, None, :, None] * w[:, :, None, :]).reshape(n * bw, n * bw)


def kernel(x_prompt, x_sample, cache_k, cache_v, state_lru, c, c_ctx, w_mod, b_mod, g_pre_mix, g_post_mix, g_pre_ffn, g_post_ffn, w_in, w_out, w_lambda, g_subln, conv_w, conv_b, lru_wa, lru_ba, lru_wx, lru_bx, lru_lambda, w_gate_up, w_down):
    n_ctx, ctx_len, _ = x_prompt.shape
    n_lat, lat_len, _ = x_sample.shape
    past = cache_k.shape[2]

    cond = jnp.concatenate(
        [c_ctx[None, :], c, jnp.zeros((COND_ROWS - 1 - n_lat, D_MODEL), F32)], axis=0)
    mod_all = _modulation(cond, w_mod, b_mod).reshape(DEPTH, COND_ROWS, N_MOD, D_MODEL)

    rope_tabs = _rope_tables(lat_len)
    chan_dft = _chan_dft_table()
    dft_ctx = _seq_dft_tables(ctx_len)
    dft_lat = _seq_dft_tables(lat_len)
    ck = cache_k.reshape(n_lat, DEPTH, past, ATTN_W)
    cv = cache_v.reshape(n_lat, DEPTH, past, ATTN_W)
    h0_ctx = jnp.zeros((n_ctx, 2, LRU_W), F32)

    xp = x_prompt.reshape(1, n_ctx * ctx_len, D_MODEL)
    xs = x_sample
    new_k, new_v, new_h = [], [], []
    for l in range(DEPTH):
        li = _lambda_init(l)
        mod = mod_all[l]
        w_in_l = w_in[l].astype(BF16)
        w_out_l = w_out[l].astype(BF16)
        w_gu_l = w_gate_up[l].astype(BF16)
        w_down_l = w_down[l].astype(BF16)
        w_gates = jnp.concatenate(
            [_block_diag(lru_wa[l, 0]), _block_diag(lru_wx[l, 0]),
             _block_diag(lru_wa[l, 1]), _block_diag(lru_wx[l, 1])], axis=1).astype(BF16)
        b_gates = jnp.concatenate(
            [lru_ba[l, 0], lru_bx[l, 0], lru_ba[l, 1], lru_bx[l, 1]])[None, :]
        vecs = [g[l][None, :] for g in (g_post_mix, g_pre_ffn, g_post_ffn)]

        qt, k, vt, rest, k_f32, v_f32 = _inproj(
            xp, mod, 0, g_pre_mix[l][None, :], w_in_l, None, emit_kv=True)
        new_k.append(k_f32.reshape(n_ctx, ctx_len, N_HEADS, 2, HEAD_DIM))
        new_v.append(v_f32.reshape(n_ctx, ctx_len, N_HEADS, HEAD_W))
        attn = _attention(qt, k, vt, None, w_lambda[l], g_subln[l], li,
                          n_seq=n_ctx, seq_len=ctx_len, tq=ctx_len)
        rest_seq = rest.reshape(n_ctx, ctx_len, REST_W)
        lru, h_fin = _rg_lru(rest_seq, conv_w[l], conv_b[l], w_gates, b_gates,
                             lru_lambda[l], h0_ctx)
        new_h.append(h_fin)
        four = _fourier(rest_seq, chan_dft, *dft_ctx)
        four_spec = pl.BlockSpec((ctx_len, FOURIER_W), lambda g, i: (0, i))
        xp = _post(xp, attn, lru.reshape(1, n_ctx * ctx_len, LRU_W), four, four_spec, mod, 0,
                   *vecs, w_out_l, w_gu_l, w_down_l, tm=ctx_len)

        qt, k, vt, rest = _inproj(
            xs, mod, 1, g_pre_mix[l][None, :], w_in_l, rope_tabs, emit_kv=False)
        attn = _attention(qt, k, vt, (ck, cv, l), w_lambda[l], g_subln[l], li,
                          n_seq=n_lat, seq_len=lat_len, tq=256)
        lru, _ = _rg_lru(rest, conv_w[l], conv_b[l], w_gates, b_gates,
                         lru_lambda[l], state_lru[:, l])
        four = _fourier(rest, chan_dft, *dft_lat)
        tm = 256
        four_spec = pl.BlockSpec((tm, FOURIER_W), lambda g, i: (i, g))
        xs = _post(xs, attn, lru, four, four_spec, mod, 1,
                   *vecs, w_out_l, w_gu_l, w_down_l, tm=tm)

    return (xp.reshape(n_ctx, ctx_len, D_MODEL), xs,
            jnp.stack(new_k, axis=1), jnp.stack(new_v, axis=1), jnp.stack(new_h, axis=1))
```

```python
import functools
import math

import jax
import jax.numpy as jnp
import numpy as np
from jax import lax
from jax.experimental import pallas as pl
from jax.experimental.pallas import tpu as pltpu

D_MODEL = 1024
DEPTH = 2
GRID_W = 64
HEAD_DIM = 64
N_HEADS = 4
HEAD_W = 2 * HEAD_DIM
ATTN_W = N_HEADS * HEAD_W
LRU_W = D_MODEL // 4
LRU_BLOCKS = 4
CONV_W = 4
FOURIER_W = D_MODEL // 4
FOURIER_GROUPS = 4
FOURIER_GW = FOURIER_W // FOURIER_GROUPS
MIX_W = ATTN_W + LRU_W + FOURIER_W
REST_W = 2 * LRU_W + FOURIER_W
IN_W = 3 * ATTN_W + REST_W
FFN_HIDDEN = -(-8 * D_MODEL // (3 * 256)) * 256
ROPE_BASE = 10000.0
RG_C = 8.0
EPS = 1e-6
N_MOD = 6
COND_ROWS = 8

F32 = jnp.float32
BF16 = jnp.bfloat16
V7X_VMEM_LIMIT_BYTES = 56 * 1024 * 1024
CONV_HALO = 8


def _cparams(*semantics):
    return pltpu.CompilerParams(dimension_semantics=semantics,
                                vmem_limit_bytes=V7X_VMEM_LIMIT_BYTES)


def _resident(shape, index_map):
    return pl.BlockSpec(shape, index_map, pipeline_mode=pl.Buffered(1))


def _lambda_init(l):
    return 0.8 - 0.6 * math.exp(-0.3 * l)


def _rms(x, axis):
    return x * lax.rsqrt(jnp.mean(x * x, axis=axis, keepdims=True) + EPS)


def _mod_kernel(cond_ref, w_ref, b_ref, o_ref):
    s = cond_ref[...]
    s = s * jax.nn.sigmoid(s)
    o_ref[0] = jnp.dot(s, w_ref[0], precision=lax.Precision.HIGHEST,
                       preferred_element_type=F32) + b_ref[0]


def _modulation(cond, w_mod, b_mod):
    tn = 1536
    n_out = N_MOD * D_MODEL
    return pl.pallas_call(
        _mod_kernel,
        grid=(DEPTH, n_out // tn),
        in_specs=[pl.BlockSpec((COND_ROWS, D_MODEL), lambda l, j: (0, 0)),
                  pl.BlockSpec((1, D_MODEL, tn), lambda l, j: (l, 0, j)),
                  pl.BlockSpec((1, 1, tn), lambda l, j: (l, 0, j))],
        out_specs=pl.BlockSpec((1, COND_ROWS, tn), lambda l, j: (l, 0, j)),
        out_shape=jax.ShapeDtypeStruct((DEPTH, COND_ROWS, n_out), F32),
        compiler_params=_cparams("arbitrary", "arbitrary"),
        name="modulation",
    )(cond, w_mod, b_mod.reshape(DEPTH, 1, n_out))


def _inproj_kernel(*refs, rope, emit_kv, tm):
    x_ref, mod_ref, g_ref, w_ref = refs[:4]
    refs = refs[4:]
    if rope:
        cos_ref, sin_lo_ref, sin_hi_ref = refs[:3]
        refs = refs[3:]
    qt_ref, k_ref, vt_ref, rest_ref = refs[:4]
    x = x_ref[0]
    mod = mod_ref[0]
    h = (_rms(x, -1) * g_ref[...]) * (1.0 + mod[1:2]) + mod[0:1]
    proj = jnp.dot(h.astype(BF16), w_ref[...], preferred_element_type=F32)
    q = proj[:, :ATTN_W]
    k = proj[:, ATTN_W:2 * ATTN_W]
    v = proj[:, 2 * ATTN_W:3 * ATTN_W]
    if emit_kv:
        kf_ref, vf_ref = refs[4:6]
        kf_ref[...] = k
        vf_ref[...] = v

    def rotary(t):
        return (t * cos_ref[...] + pltpu.roll(t, HEAD_W - 16, 1) * sin_lo_ref[...]
                + pltpu.roll(t, 16, 1) * sin_hi_ref[...])

    row = lax.broadcasted_iota(jnp.int32, (HEAD_W, tm), 0)
    scale = HEAD_DIM ** -0.5 * math.log2(math.e)
    for hd in range(N_HEADS):
        qh = q[:, hd * HEAD_W:(hd + 1) * HEAD_W]
        kh = k[:, hd * HEAD_W:(hd + 1) * HEAD_W]
        if rope:
            qh = rotary(qh)
            kh = rotary(kh)
        qht = (qh * scale).T
        qt_ref[0, 2 * hd] = jnp.where(row < HEAD_DIM, qht, 0.0).astype(BF16)
        qt_ref[0, 2 * hd + 1] = jnp.where(row >= HEAD_DIM, qht, 0.0).astype(BF16)
        k_ref[0, :, hd * HEAD_W:(hd + 1) * HEAD_W] = kh.astype(BF16)
    vt_ref[0] = v.T.astype(BF16)
    rest_ref[0] = proj[:, 3 * ATTN_W:]


def _inproj(x, mod, goff, g_pre, w_in, rope_tabs, emit_kv):
    G, T, _ = x.shape
    tm = 512
    rope = rope_tabs is not None
    in_specs = [pl.BlockSpec((1, tm, D_MODEL), lambda g, i: (g, i, 0)),
                pl.BlockSpec((1, N_MOD, D_MODEL), lambda g, i: (g + goff, 0, 0)),
                pl.BlockSpec((1, D_MODEL), lambda g, i: (0, 0)),
                _resident((D_MODEL, IN_W), lambda g, i: (0, 0))]
    args = [x, mod, g_pre, w_in]
    if rope:
        in_specs += [pl.BlockSpec((tm, HEAD_W), lambda g, i: (i, 0))] * 3
        args += list(rope_tabs)
    out_specs = [pl.BlockSpec((1, 2 * N_HEADS, HEAD_W, tm), lambda g, i: (g, 0, 0, i)),
                 pl.BlockSpec((1, tm, ATTN_W), lambda g, i: (g, i, 0)),
                 pl.BlockSpec((1, ATTN_W, tm), lambda g, i: (g, 0, i)),
                 pl.BlockSpec((1, tm, REST_W), lambda g, i: (g, i, 0))]
    out_shape = [jax.ShapeDtypeStruct((G, 2 * N_HEADS, HEAD_W, T), BF16),
                 jax.ShapeDtypeStruct((G, T, ATTN_W), BF16),
                 jax.ShapeDtypeStruct((G, ATTN_W, T), BF16),
                 jax.ShapeDtypeStruct((G, T, REST_W), F32)]
    if emit_kv:
        assert G == 1
        out_specs += [pl.BlockSpec((tm, ATTN_W), lambda g, i: (i, 0))] * 2
        out_shape += [jax.ShapeDtypeStruct((T, ATTN_W), F32)] * 2
    return pl.pallas_call(
        functools.partial(_inproj_kernel, rope=rope, emit_kv=emit_kv, tm=tm),
        grid=(G, T // tm),
        in_specs=in_specs, out_specs=out_specs, out_shape=out_shape,
        compiler_params=_cparams("arbitrary", "arbitrary"),
        name="inproj_rope" if rope else "inproj_ctx",
    )(*args)


QUERY_BLOCK = 256
KEY_CHUNK = 256
SUBLANES = 8


ROW_ACCUMULATORS = 2


def _accumulate_rows(accs, x, op):
    for idx, r in enumerate(range(0, x.shape[0], SUBLANES)):
        j = idx % ROW_ACCUMULATORS
        part = x[r:r + SUBLANES]
        accs[j] = part if accs[j] is None else op(accs[j], part)


def _finish_rows(accs, op, reduce):
    total = accs[0]
    for a in accs[1:]:
        total = op(total, a)
    return reduce(total, axis=0, keepdims=True)


def _attn_kernel(*refs, has_cache, li, n_heads, n_sub, n_keys):
    qt_ref, k_ref, vt_ref = refs[:3]
    refs = refs[3:]
    if has_cache:
        ck_ref, cv_ref = refs[:2]
        refs = refs[2:]
    wl_ref, g_ref, o_ref = refs[:3]
    s_bufs = refs[3:5]
    n_self = k_ref.shape[1]
    if has_cache:
        kall, vtall = refs[5:7]

        @pl.when(pl.program_id(2) == 0)
        def _():
            kall[0:n_self] = k_ref[0]
            kall[n_self:n_keys] = ck_ref[0].astype(BF16)
            vtall[:, 0:n_self] = vt_ref[0]
            vtall[:, n_self:n_keys] = cv_ref[0].T.astype(BF16)

    def k_chunk(hd, c):
        rows = slice(c * KEY_CHUNK, (c + 1) * KEY_CHUNK)
        return kall[rows, :] if has_cache else k_ref[0, rows, hd * HEAD_W:(hd + 1) * HEAD_W]

    def vt_chunk(hd, c):
        cols = slice(c * KEY_CHUNK, (c + 1) * KEY_CHUNK)
        return vtall[:, cols] if has_cache else vt_ref[0, hd * HEAD_W:(hd + 1) * HEAD_W, cols]

    units = [(hd, sub, m) for hd in range(n_heads) for sub in range(n_sub) for m in range(2)]
    n_chunks = n_keys // KEY_CHUNK
    results = {}
    col_max = None
    for t in range(len(units) + 1):
        if t < len(units):
            hd, sub, m = units[t]
            qt = qt_ref[0, 2 * hd + m, :, sub * QUERY_BLOCK:(sub + 1) * QUERY_BLOCK]
            new_max = [None] * ROW_ACCUMULATORS
        if t > 0:
            hd_p = units[t - 1][0]
            mx = _finish_rows(col_max, jnp.maximum, jnp.max)
            den = [None] * ROW_ACCUMULATORS
            acc = None
        for c in range(n_chunks):
            rows = slice(c * KEY_CHUNK, (c + 1) * KEY_CHUNK)
            if t < len(units):
                s = jnp.dot(k_chunk(hd, c), qt, preferred_element_type=F32)
                s_bufs[t % 2][rows, :] = s
                _accumulate_rows(new_max, s, jnp.maximum)
            if t > 0:
                e = jnp.exp2(s_bufs[(t - 1) % 2][rows, :] - mx)
                _accumulate_rows(den, e, jnp.add)
                pv = jnp.dot(vt_chunk(hd_p, c), e.astype(BF16), preferred_element_type=F32)
                acc = pv if acc is None else acc + pv
        if t > 0:
            results[units[t - 1]] = acc / _finish_rows(den, jnp.add, jnp.sum)
        col_max = new_max

    wl = wl_ref[...]
    lam = (jnp.exp(jnp.sum(wl[0:1] * wl[1:2], axis=-1, keepdims=True))
           - jnp.exp(jnp.sum(wl[2:3] * wl[3:4], axis=-1, keepdims=True)) + li)
    for hd in range(n_heads):
        for sub in range(n_sub):
            ot = results[(hd, sub, 0)] - lam * results[(hd, sub, 1)]
            y = (_rms(ot, 0) * g_ref[...]) * (1.0 - li)
            o_ref[0, sub * QUERY_BLOCK:(sub + 1) * QUERY_BLOCK,
                  hd * HEAD_W:(hd + 1) * HEAD_W] = y.T.astype(BF16)


def _attention(qt, k, vt, cache, w_lambda_l, g_subln_l, li, n_seq, seq_len, n_heads, n_sub):
    G, _, _, T = qt.shape
    per_g = T // seq_len
    tq = n_sub * QUERY_BLOCK
    nq = seq_len // tq
    has_cache = cache is not None
    n_keys = seq_len

    def gi(b):
        return b // per_g

    def si(b):
        return b % per_g

    hw = n_heads * HEAD_W
    in_specs = [pl.BlockSpec((1, 2 * n_heads, HEAD_W, tq), lambda b, h, i: (gi(b), h, 0, si(b) * nq + i)),
                pl.BlockSpec((1, seq_len, hw), lambda b, h, i: (gi(b), si(b), h)),
                pl.BlockSpec((1, hw, seq_len), lambda b, h, i: (gi(b), h, si(b)))]
    args = [qt, k, vt]
    if has_cache:
        assert n_heads == 1
        ck, cv, layer = cache
        past = ck.shape[2]
        n_keys = seq_len + past
        in_specs += [pl.BlockSpec((1, None, past, HEAD_W), lambda b, h, i: (b, layer, 0, h))] * 2
        args += [ck, cv]
    scratch = [pltpu.VMEM((n_keys, QUERY_BLOCK), F32)] * 2
    if has_cache:
        scratch += [pltpu.VMEM((n_keys, HEAD_W), BF16), pltpu.VMEM((HEAD_W, n_keys), BF16)]
    in_specs += [pl.BlockSpec((4, HEAD_DIM), lambda b, h, i: (0, 0)),
                 pl.BlockSpec((HEAD_W, 1), lambda b, h, i: (0, 0))]
    args += [w_lambda_l, g_subln_l.reshape(HEAD_W, 1)]
    return pl.pallas_call(
        functools.partial(_attn_kernel, has_cache=has_cache, li=li, n_heads=n_heads,
                          n_sub=n_sub, n_keys=n_keys),
        grid=(n_seq, N_HEADS // n_heads, nq),
        in_specs=in_specs,
        out_specs=pl.BlockSpec((1, tq, hw), lambda b, h, i: (gi(b), si(b) * nq + i, h)),
        out_shape=jax.ShapeDtypeStruct((G, T, ATTN_W), BF16),
        scratch_shapes=scratch,
        compiler_params=_cparams("arbitrary", "arbitrary", "arbitrary"),
        name="attn_latent" if has_cache else "attn_ctx",
    )(*args)


def _scan_chunk(a, b, row, reverse):
    n = a.shape[0]
    d = 1
    while d < n:
        if reverse:
            shift, valid = n - d, row < n - d
        else:
            shift, valid = d, row >= d
        a_prev = jnp.where(valid, pltpu.roll(a, shift, 0), 1.0)
        b_prev = jnp.where(valid, pltpu.roll(b, shift, 0), 0.0)
        b = a * b_prev + b
        a = a * a_prev
        d *= 2
    return a, b


def _lru_kernel(xr_ref, gr_ref, cw_ref, cb_ref, w_ref, bias_ref, lam_ref, h0_ref,
                out_ref, hfin_ref, xpad, hf_scr, ab_scr, bb_scr, *, seq_len, tc):
    nch = seq_len // tc
    zeros_halo = jnp.zeros((CONV_HALO, LRU_W), F32)
    xpad[0:CONV_HALO] = zeros_halo
    xpad[seq_len + CONV_HALO:seq_len + 2 * CONV_HALO] = zeros_halo
    xpad[CONV_HALO:seq_len + CONV_HALO] = xr_ref[0]
    lam_p = lam_ref[...]
    neg = -lam_p
    softplus = jnp.maximum(neg, 0.0) + jnp.log1p(jnp.exp(-jnp.abs(neg)))
    coef = -RG_C * softplus
    cw = cw_ref[...]
    row = lax.broadcasted_iota(jnp.int32, (tc, LRU_W), 0)
    h0 = h0_ref[0]

    def decay_and_input(xc, z, direction):
        off = direction * 2 * LRU_W
        r = jax.nn.sigmoid(z[:, off:off + LRU_W])
        i = jax.nn.sigmoid(z[:, off + LRU_W:off + 2 * LRU_W])
        log_a = coef[direction:direction + 1] * r
        a = jnp.exp(log_a)
        b = jnp.sqrt(-jnp.tanh(log_a) * (1.0 + a * a)) * (i * xc)
        return a, b

    def forward_chunk(c, carry):
        start = pl.multiple_of(c * tc, tc)
        win = xpad[pl.ds(start, tc + 2 * CONV_HALO), :]
        lo = CONV_HALO - CONV_W // 2
        xc = win[lo:lo + tc] * cw[0:1]
        for j in range(1, CONV_W):
            xc = xc + win[lo + j:lo + j + tc] * cw[j:j + 1]
        xc = xc + cb_ref[...]
        z = jnp.dot(xc.astype(BF16), w_ref[...], preferred_element_type=F32) + bias_ref[...]
        a_f, b_f = decay_and_input(xc, z, 0)
        a_cum, h_loc = _scan_chunk(a_f, b_f, row, reverse=False)
        hf = a_cum * carry + h_loc
        hf_scr[pl.ds(start, tc), :] = hf
        a_b, b_b = decay_and_input(xc, z, 1)
        ab_scr[pl.ds(start, tc), :] = a_b
        bb_scr[pl.ds(start, tc), :] = b_b
        return hf[tc - 1:tc]

    def backward_chunk(j, carry):
        c = nch - 1 - j
        start = pl.multiple_of(c * tc, tc)
        a_cum, h_loc = _scan_chunk(ab_scr[pl.ds(start, tc), :], bb_scr[pl.ds(start, tc), :],
                                   row, reverse=True)
        hb = a_cum * carry + h_loc
        gate = jax.nn.gelu(gr_ref[0, pl.ds(start, tc), :], approximate=True)
        out_ref[0, pl.ds(start, tc), :] = ((hf_scr[pl.ds(start, tc), :] + hb) * gate).astype(BF16)
        return hb[0:1]

    if nch == 1:
        hf_fin = forward_chunk(0, h0[0:1])
        hb_fin = backward_chunk(0, h0[1:2])
    else:
        hf_fin = lax.fori_loop(0, nch, forward_chunk, h0[0:1])
        hb_fin = lax.fori_loop(0, nch, backward_chunk, h0[1:2])
    hfin_ref[0, 0:1] = hf_fin
    hfin_ref[0, 1:2] = hb_fin


def _rg_lru(rest, conv_w, conv_b, w_gates, b_gates, lam_p, h0):
    B, L, _ = rest.shape
    tc = min(L, 256)
    return pl.pallas_call(
        functools.partial(_lru_kernel, seq_len=L, tc=tc),
        grid=(B,),
        in_specs=[pl.BlockSpec((1, L, LRU_W), lambda b: (b, 0, 0)),
                  pl.BlockSpec((1, L, LRU_W), lambda b: (b, 0, 1)),
                  pl.BlockSpec((CONV_W, LRU_W), lambda b: (0, 0)),
                  pl.BlockSpec((1, LRU_W), lambda b: (0, 0)),
                  pl.BlockSpec((LRU_W, 4 * LRU_W), lambda b: (0, 0)),
                  pl.BlockSpec((1, 4 * LRU_W), lambda b: (0, 0)),
                  pl.BlockSpec((2, LRU_W), lambda b: (0, 0)),
                  pl.BlockSpec((1, 2, LRU_W), lambda b: (b, 0, 0))],
        out_specs=[pl.BlockSpec((1, L, LRU_W), lambda b: (b, 0, 0)),
                   pl.BlockSpec((1, 2, LRU_W), lambda b: (b, 0, 0))],
        out_shape=[jax.ShapeDtypeStruct((B, L, LRU_W), BF16),
                   jax.ShapeDtypeStruct((B, 2, LRU_W), F32)],
        scratch_shapes=[pltpu.VMEM((L + 2 * CONV_HALO, LRU_W), F32),
                        pltpu.VMEM((L, LRU_W), F32),
                        pltpu.VMEM((L, LRU_W), F32),
                        pltpu.VMEM((L, LRU_W), F32)],
        compiler_params=_cparams("arbitrary"),
        name="rg_lru",
    )(rest, rest, conv_w, conv_b.reshape(1, LRU_W), w_gates, b_gates, lam_p, h0)


def _chan_dft_kernel(x_ref, w_ref, yc_ref, ys_ref):
    y = jnp.dot(x_ref[0].astype(BF16), w_ref[...], preferred_element_type=F32)
    yc_ref[...] = y[:, :FOURIER_W].astype(BF16)
    ys_ref[...] = y[:, FOURIER_W:].astype(BF16)


def _seq_dft_kernel(fc_ref, fs_ref, yc_ref, ys_ref, o_ref, *, scale):
    acc = jnp.dot(fc_ref[...], yc_ref[...], preferred_element_type=F32)
    acc = acc - jnp.dot(fs_ref[...], ys_ref[...], preferred_element_type=F32)
    o_ref[...] = (acc * scale).astype(BF16)


def _fourier(rest, chan_dft, seq_cos, seq_sin):
    B, L, _ = rest.shape
    tm = min(L, 512)
    xf_block = (2 * LRU_W) // FOURIER_W
    y_shape = jax.ShapeDtypeStruct((L, B * FOURIER_W), BF16)
    yc, ys = pl.pallas_call(
        _chan_dft_kernel,
        grid=(B, L // tm),
        in_specs=[pl.BlockSpec((1, tm, FOURIER_W), lambda b, i: (b, i, xf_block)),
                  pl.BlockSpec((FOURIER_W, 2 * FOURIER_W), lambda b, i: (0, 0))],
        out_specs=[pl.BlockSpec((tm, FOURIER_W), lambda b, i: (i, b))] * 2,
        out_shape=[y_shape, y_shape],
        compiler_params=_cparams("arbitrary", "arbitrary"),
        name="fourier_channels",
    )(rest, chan_dft)
    tr = min(L, 256)
    return pl.pallas_call(
        functools.partial(_seq_dft_kernel, scale=(L * FOURIER_GW) ** -0.5),
        grid=(L // tr,),
        in_specs=[pl.BlockSpec((tr, L), lambda i: (i, 0)),
                  pl.BlockSpec((tr, L), lambda i: (i, 0)),
                  _resident((L, B * FOURIER_W), lambda i: (0, 0)),
                  _resident((L, B * FOURIER_W), lambda i: (0, 0))],
        out_specs=pl.BlockSpec((tr, B * FOURIER_W), lambda i: (i, 0)),
        out_shape=y_shape,
        compiler_params=_cparams("arbitrary"),
        name="fourier_sequence",
    )(seq_cos, seq_sin, yc, ys)


def _post_kernel(x_ref, attn_ref, lru_ref, four_ref, mod_ref, gmix_ref, gpre_ref, gffn_ref,
                 wo_ref, wgu_ref, wd_ref, o_ref):
    mod = mod_ref[0]
    o = jnp.dot(attn_ref[0], wo_ref[0:ATTN_W], preferred_element_type=F32)
    o = o + jnp.dot(lru_ref[0], wo_ref[ATTN_W:ATTN_W + LRU_W], preferred_element_type=F32)
    o = o + jnp.dot(four_ref[...], wo_ref[ATTN_W + LRU_W:MIX_W], preferred_element_type=F32)
    x1 = x_ref[0] + mod[2:3] * (_rms(o, -1) * gmix_ref[...])
    h = (_rms(x1, -1) * gpre_ref[...]) * (1.0 + mod[4:5]) + mod[3:4]
    hb = h.astype(BF16)
    gt = jnp.dot(hb, wgu_ref[:, :FFN_HIDDEN], preferred_element_type=F32)
    up = jnp.dot(hb, wgu_ref[:, FFN_HIDDEN:], preferred_element_type=F32)
    act = (gt * jax.nn.sigmoid(gt)) * up
    f = jnp.dot(act.astype(BF16), wd_ref[...], preferred_element_type=F32)
    o_ref[0] = x1 + mod[5:6] * (_rms(f, -1) * gffn_ref[...])


def _post(x, attn, lru, four, four_spec, mod, goff, g_post_mix, g_pre_ffn, g_post_ffn,
          w_out, w_gate_up, w_down, tm):
    G, T, _ = x.shape
    vec = pl.BlockSpec((1, D_MODEL), lambda g, i: (0, 0))
    return pl.pallas_call(
        _post_kernel,
        grid=(G, T // tm),
        in_specs=[pl.BlockSpec((1, tm, D_MODEL), lambda g, i: (g, i, 0)),
                  pl.BlockSpec((1, tm, ATTN_W), lambda g, i: (g, i, 0)),
                  pl.BlockSpec((1, tm, LRU_W), lambda g, i: (g, i, 0)),
                  four_spec,
                  pl.BlockSpec((1, N_MOD, D_MODEL), lambda g, i: (g + goff, 0, 0)),
                  vec, vec, vec,
                  _resident((MIX_W, D_MODEL), lambda g, i: (0, 0)),
                  _resident((D_MODEL, 2 * FFN_HIDDEN), lambda g, i: (0, 0)),
                  _resident((FFN_HIDDEN, D_MODEL), lambda g, i: (0, 0))],
        out_specs=pl.BlockSpec((1, tm, D_MODEL), lambda g, i: (g, i, 0)),
        out_shape=jax.ShapeDtypeStruct((G, T, D_MODEL), F32),
        compiler_params=_cparams("arbitrary", "arbitrary"),
        name="post",
    )(x, attn, lru, four, mod, g_post_mix, g_pre_ffn, g_post_ffn, w_out, w_gate_up, w_down)


def _rope_tables(n_tokens):
    rows = n_tokens // GRID_W
    row = jnp.repeat(jnp.arange(rows, dtype=F32), GRID_W)
    col = jnp.tile(jnp.arange(GRID_W, dtype=F32), rows)
    n = HEAD_DIM // 4
    inv = ROPE_BASE ** (-jnp.arange(n, dtype=F32) / n)
    ang_r, ang_c = row[:, None] * inv, col[:, None] * inv
    zero = jnp.zeros_like(ang_r)
    cos64 = jnp.concatenate([jnp.cos(ang_r)] * 2 + [jnp.cos(ang_c)] * 2, axis=-1)
    sin_lo = jnp.concatenate([-jnp.sin(ang_r), zero, -jnp.sin(ang_c), zero], axis=-1)
    sin_hi = jnp.concatenate([zero, jnp.sin(ang_r), zero, jnp.sin(ang_c)], axis=-1)
    return tuple(jnp.tile(t, (1, 2)) for t in (cos64, sin_lo, sin_hi))


def _seq_dft_tables(n):
    r = 1 << (n.bit_length() // 2)
    l = jnp.arange(n, dtype=jnp.int32)

    def cos_sin(rows, period):
        k = jnp.arange(rows, dtype=jnp.int32)
        ang = ((k[:, None] * l[None, :]) % period).astype(F32) * (2.0 * math.pi / period)
        return jnp.cos(ang), jnp.sin(ang)

    ca, sa = cos_sin(n // r, n // r)
    cb, sb = cos_sin(r, n)
    cos = ca[:, None, :] * cb[None] - sa[:, None, :] * sb[None]
    sin = sa[:, None, :] * cb[None] + ca[:, None, :] * sb[None]
    return cos.reshape(n, n).astype(BF16), sin.reshape(n, n).astype(BF16)


def _chan_dft_table():
    k = np.arange(FOURIER_GW)
    ang = 2.0 * np.pi * ((k[:, None] * k[None, :]) % FOURIER_GW) / FOURIER_GW
    eye = np.eye(FOURIER_GROUPS)
    table = np.concatenate([np.kron(eye, np.cos(ang)), np.kron(eye, np.sin(ang))], axis=1)
    return jnp.asarray(table, dtype=F32).astype(BF16)


def _block_diag(w):
    n, bw, _ = w.shape
    eye = jnp.eye(n, dtype=w.dtype)
    return (eye[:, None, :, None] * w[:, :, None, :]).reshape(n * bw, n * bw)


def kernel(x_prompt, x_sample, cache_k, cache_v, state_lru, c, c_ctx, w_mod, b_mod, g_pre_mix, g_post_mix, g_pre_ffn, g_post_ffn, w_in, w_out, w_lambda, g_subln, conv_w, conv_b, lru_wa, lru_ba, lru_wx, lru_bx, lru_lambda, w_gate_up, w_down):
    n_ctx, ctx_len, _ = x_prompt.shape
    n_lat, lat_len, _ = x_sample.shape
    past = cache_k.shape[2]

    cond = jnp.concatenate(
        [c_ctx[None, :], c, jnp.zeros((COND_ROWS - 1 - n_lat, D_MODEL), F32)], axis=0)
    mod_all = _modulation(cond, w_mod, b_mod).reshape(DEPTH, COND_ROWS, N_MOD, D_MODEL)

    rope_tabs = _rope_tables(lat_len)
    chan_dft = _chan_dft_table()
    dft_ctx = _seq_dft_tables(ctx_len)
    dft_lat = _seq_dft_tables(lat_len)
    ck = cache_k.reshape(n_lat, DEPTH, past, ATTN_W)
    cv = cache_v.reshape(n_lat, DEPTH, past, ATTN_W)
    h0_ctx = jnp.zeros((n_ctx, 2, LRU_W), F32)

    xp = x_prompt.reshape(1, n_ctx * ctx_len, D_MODEL)
    xs = x_sample
    new_k, new_v, new_h = [], [], []
    for l in range(DEPTH):
        li = _lambda_init(l)
        mod = mod_all[l]
        w_in_l = w_in[l].astype(BF16)
        w_out_l = w_out[l].astype(BF16)
        w_gu_l = w_gate_up[l].astype(BF16)
        w_down_l = w_down[l].astype(BF16)
        w_gates = jnp.concatenate(
            [_block_diag(lru_wa[l, 0]), _block_diag(lru_wx[l, 0]),
             _block_diag(lru_wa[l, 1]), _block_diag(lru_wx[l, 1])], axis=1).astype(BF16)
        b_gates = jnp.concatenate(
            [lru_ba[l, 0], lru_bx[l, 0], lru_ba[l, 1], lru_bx[l, 1]])[None, :]
        vecs = [g[l][None, :] for g in (g_post_mix, g_pre_ffn, g_post_ffn)]

        qt, k, vt, rest, k_f32, v_f32 = _inproj(
            xp, mod, 0, g_pre_mix[l][None, :], w_in_l, None, emit_kv=True)
        new_k.append(k_f32.reshape(n_ctx, ctx_len, N_HEADS, 2, HEAD_DIM))
        new_v.append(v_f32.reshape(n_ctx, ctx_len, N_HEADS, HEAD_W))
        attn = _attention(qt, k, vt, None, w_lambda[l], g_subln[l], li,
                          n_seq=n_ctx, seq_len=ctx_len, n_heads=N_HEADS,
                          n_sub=ctx_len // QUERY_BLOCK)
        rest_seq = rest.reshape(n_ctx, ctx_len, REST_W)
        lru, h_fin = _rg_lru(rest_seq, conv_w[l], conv_b[l], w_gates, b_gates,
                             lru_lambda[l], h0_ctx)
        new_h.append(h_fin)
        four = _fourier(rest_seq, chan_dft, *dft_ctx)
        four_spec = pl.BlockSpec((ctx_len, FOURIER_W), lambda g, i: (0, i))
        xp = _post(xp, attn, lru.reshape(1, n_ctx * ctx_len, LRU_W), four, four_spec, mod, 0,
                   *vecs, w_out_l, w_gu_l, w_down_l, tm=ctx_len)

        qt, k, vt, rest = _inproj(
            xs, mod, 1, g_pre_mix[l][None, :], w_in_l, rope_tabs, emit_kv=False)
        attn = _attention(qt, k, vt, (ck, cv, l), w_lambda[l], g_subln[l], li,
                          n_seq=n_lat, seq_len=lat_len, n_heads=1, n_sub=4)
        lru, _ = _rg_lru(rest, conv_w[l], conv_b[l], w_gates, b_gates,
                         lru_lambda[l], state_lru[:, l])
        four = _fourier(rest, chan_dft, *dft_lat)
        tm = 256
        four_spec = pl.BlockSpec((tm, FOURIER_W), lambda g, i: (i, g))
        xs = _post(xs, attn, lru, four, four_spec, mod, 1,
                   *vecs, w_out_l, w_gu_l, w_down_l, tm=tm)

    return (xp.reshape(n_ctx, ctx_len, D_MODEL), xs,
            jnp.stack(new_k, axis=1), jnp.stack(new_v, axis=1), jnp.stack(new_h, axis=1))
```

```python
import functools
import math

import jax
import jax.numpy as jnp
import numpy as np
from jax import lax
from jax.experimental import pallas as pl
from jax.experimental.pallas import tpu as pltpu

D_MODEL = 1024
DEPTH = 2
GRID_W = 64
HEAD_DIM = 64
N_HEADS = 4
HEAD_W = 2 * HEAD_DIM
ATTN_W = N_HEADS * HEAD_W
LRU_W = D_MODEL // 4
LRU_BLOCKS = 4
CONV_W = 4
FOURIER_W = D_MODEL // 4
FOURIER_GROUPS = 4
FOURIER_GW = FOURIER_W // FOURIER_GROUPS
MIX_W = ATTN_W + LRU_W + FOURIER_W
REST_W = 2 * LRU_W
IN_W = 3 * ATTN_W + REST_W + FOURIER_W
FFN_HIDDEN = -(-8 * D_MODEL // (3 * 256)) * 256
ROPE_BASE = 10000.0
RG_C = 8.0
EPS = 1e-6
N_MOD = 6
COND_ROWS = 8

F32 = jnp.float32
BF16 = jnp.bfloat16
V7X_VMEM_LIMIT_BYTES = 56 * 1024 * 1024
CONV_HALO = 8


def _cparams(*semantics, flags=None):
    return pltpu.CompilerParams(dimension_semantics=semantics,
                                vmem_limit_bytes=V7X_VMEM_LIMIT_BYTES, flags=flags)


def _resident(shape, index_map):
    return pl.BlockSpec(shape, index_map, pipeline_mode=pl.Buffered(1))


def _lambda_init(l):
    return 0.8 - 0.6 * math.exp(-0.3 * l)


def _rms(x, axis):
    return x * lax.rsqrt(jnp.mean(x * x, axis=axis, keepdims=True) + EPS)


def _mod_kernel(cond_ref, w_ref, b_ref, o_ref):
    s = cond_ref[...]
    s = s * jax.nn.sigmoid(s)
    o_ref[0] = jnp.dot(s, w_ref[0], precision=lax.Precision.HIGHEST,
                       preferred_element_type=F32) + b_ref[0]


def _modulation(cond, w_mod, b_mod):
    tn = 1536
    n_out = N_MOD * D_MODEL
    return pl.pallas_call(
        _mod_kernel,
        grid=(DEPTH, n_out // tn),
        in_specs=[pl.BlockSpec((COND_ROWS, D_MODEL), lambda l, j: (0, 0)),
                  pl.BlockSpec((1, D_MODEL, tn), lambda l, j: (l, 0, j)),
                  pl.BlockSpec((1, 1, tn), lambda l, j: (l, 0, j))],
        out_specs=pl.BlockSpec((1, COND_ROWS, tn), lambda l, j: (l, 0, j)),
        out_shape=jax.ShapeDtypeStruct((DEPTH, COND_ROWS, n_out), F32),
        compiler_params=_cparams("arbitrary", "arbitrary"),
        name="modulation",
    )(cond, w_mod, b_mod.reshape(DEPTH, 1, n_out))


def _inproj_kernel(*refs, rope, emit_kv, tm):
    x_ref, mod_ref, g_ref, w_ref = refs[:4]
    refs = refs[4:]
    if rope:
        cos_ref, sin_lo_ref, sin_hi_ref = refs[:3]
        refs = refs[3:]
    qt_ref, k_ref, vt_ref, rest_ref, xf_ref = refs[:5]
    x = x_ref[0]
    mod = mod_ref[0]
    h = (_rms(x, -1) * g_ref[...]) * (1.0 + mod[1:2]) + mod[0:1]
    proj = jnp.dot(h.astype(BF16), w_ref[...], preferred_element_type=F32)
    q = proj[:, :ATTN_W]
    k = proj[:, ATTN_W:2 * ATTN_W]
    v = proj[:, 2 * ATTN_W:3 * ATTN_W]
    if emit_kv:
        kf_ref, vf_ref = refs[5:7]
        kf_ref[...] = k
        vf_ref[...] = v

    def rotary(t):
        return (t * cos_ref[...] + pltpu.roll(t, HEAD_W - 16, 1) * sin_lo_ref[...]
                + pltpu.roll(t, 16, 1) * sin_hi_ref[...])

    row = lax.broadcasted_iota(jnp.int32, (HEAD_W, tm), 0)
    scale = HEAD_DIM ** -0.5 * math.log2(math.e)
    for hd in range(N_HEADS):
        qh = q[:, hd * HEAD_W:(hd + 1) * HEAD_W]
        kh = k[:, hd * HEAD_W:(hd + 1) * HEAD_W]
        if rope:
            qh = rotary(qh)
            kh = rotary(kh)
        qht = (qh * scale).T
        qt_ref[0, 2 * hd] = jnp.where(row < HEAD_DIM, qht, 0.0).astype(BF16)
        qt_ref[0, 2 * hd + 1] = jnp.where(row >= HEAD_DIM, qht, 0.0).astype(BF16)
        k_ref[0, :, hd * HEAD_W:(hd + 1) * HEAD_W] = kh.astype(BF16)
    vt_ref[0] = v.T.astype(BF16)
    rest_ref[0] = proj[:, 3 * ATTN_W:3 * ATTN_W + REST_W]
    xf_ref[0] = proj[:, 3 * ATTN_W + REST_W:].astype(BF16)


def _inproj(x, mod, goff, g_pre, w_in, rope_tabs, emit_kv):
    G, T, _ = x.shape
    tm = 512
    rope = rope_tabs is not None
    in_specs = [pl.BlockSpec((1, tm, D_MODEL), lambda g, i: (g, i, 0)),
                pl.BlockSpec((1, N_MOD, D_MODEL), lambda g, i: (g + goff, 0, 0)),
                pl.BlockSpec((1, D_MODEL), lambda g, i: (0, 0)),
                _resident((D_MODEL, IN_W), lambda g, i: (0, 0))]
    args = [x, mod, g_pre, w_in]
    if rope:
        in_specs += [pl.BlockSpec((tm, HEAD_W), lambda g, i: (i, 0))] * 3
        args += list(rope_tabs)
    out_specs = [pl.BlockSpec((1, 2 * N_HEADS, HEAD_W, tm), lambda g, i: (g, 0, 0, i)),
                 pl.BlockSpec((1, tm, ATTN_W), lambda g, i: (g, i, 0)),
                 pl.BlockSpec((1, ATTN_W, tm), lambda g, i: (g, 0, i)),
                 pl.BlockSpec((1, tm, REST_W), lambda g, i: (g, i, 0)),
                 pl.BlockSpec((1, tm, FOURIER_W), lambda g, i: (g, i, 0))]
    out_shape = [jax.ShapeDtypeStruct((G, 2 * N_HEADS, HEAD_W, T), BF16),
                 jax.ShapeDtypeStruct((G, T, ATTN_W), BF16),
                 jax.ShapeDtypeStruct((G, ATTN_W, T), BF16),
                 jax.ShapeDtypeStruct((G, T, REST_W), F32),
                 jax.ShapeDtypeStruct((G, T, FOURIER_W), BF16)]
    if emit_kv:
        assert G == 1
        out_specs += [pl.BlockSpec((tm, ATTN_W), lambda g, i: (i, 0))] * 2
        out_shape += [jax.ShapeDtypeStruct((T, ATTN_W), F32)] * 2
    return pl.pallas_call(
        functools.partial(_inproj_kernel, rope=rope, emit_kv=emit_kv, tm=tm),
        grid=(G, T // tm),
        in_specs=in_specs, out_specs=out_specs, out_shape=out_shape,
        compiler_params=_cparams("arbitrary", "arbitrary"),
        name="inproj_rope" if rope else "inproj_ctx",
    )(*args)


QUERY_BLOCK = 256
KEY_CHUNK = 256
SUBLANES = 8
SUM_ACCUMULATORS = 2


ROW_ACCUMULATORS = 2


def _accumulate_rows(accs, x, op):
    for idx, r in enumerate(range(0, x.shape[0], SUBLANES)):
        j = idx % len(accs)
        part = x[r:r + SUBLANES]
        accs[j] = part if accs[j] is None else op(accs[j], part)


def _finish_rows(accs, op, reduce):
    total = accs[0]
    for a in accs[1:]:
        total = op(total, a)
    return reduce(total, axis=0, keepdims=True)


def _attn_kernel(*refs, has_cache, li, n_heads, n_sub, n_keys):
    qt_ref, k_ref, vt_ref = refs[:3]
    refs = refs[3:]
    if has_cache:
        ck_ref, cv_ref = refs[:2]
        refs = refs[2:]
    wl_ref, g_ref, o_ref = refs[:3]
    s_bufs = refs[3:5]
    n_self = k_ref.shape[1]
    if has_cache:
        kall, vtall = refs[5:7]

        @pl.when(pl.program_id(2) == 0)
        def _():
            kall[0:n_self] = k_ref[0]
            kall[n_self:n_keys] = ck_ref[0].astype(BF16)
            vtall[:, 0:n_self] = vt_ref[0]
            vtall[:, n_self:n_keys] = cv_ref[0].T.astype(BF16)

    def k_chunk(hd, c):
        rows = slice(c * KEY_CHUNK, (c + 1) * KEY_CHUNK)
        return kall[rows, :] if has_cache else k_ref[0, rows, hd * HEAD_W:(hd + 1) * HEAD_W]

    def vt_chunk(hd, c):
        cols = slice(c * KEY_CHUNK, (c + 1) * KEY_CHUNK)
        return vtall[:, cols] if has_cache else vt_ref[0, hd * HEAD_W:(hd + 1) * HEAD_W, cols]

    units = [(hd, sub, m) for hd in range(n_heads) for sub in range(n_sub) for m in range(2)]
    n_chunks = n_keys // KEY_CHUNK
    results = {}
    col_max = None
    for t in range(len(units) + 1):
        if t < len(units):
            hd, sub, m = units[t]
            qt = qt_ref[0, 2 * hd + m, :, sub * QUERY_BLOCK:(sub + 1) * QUERY_BLOCK]
            new_max = [None] * ROW_ACCUMULATORS
        if t > 0:
            hd_p = units[t - 1][0]
            mx = _finish_rows(col_max, jnp.maximum, jnp.max)
            den = [None] * SUM_ACCUMULATORS
            acc = None
        for c in range(n_chunks):
            rows = slice(c * KEY_CHUNK, (c + 1) * KEY_CHUNK)
            if t < len(units):
                s = jnp.dot(k_chunk(hd, c), qt, preferred_element_type=F32)
                s_bufs[t % 2][rows, :] = s
                _accumulate_rows(new_max, s, jnp.maximum)
            if t > 0:
                e = jnp.exp2(s_bufs[(t - 1) % 2][rows, :] - mx)
                _accumulate_rows(den, e, jnp.add)
                pv = jnp.dot(vt_chunk(hd_p, c), e.astype(BF16), preferred_element_type=F32)
                acc = pv if acc is None else acc + pv
        if t > 0:
            results[units[t - 1]] = acc / _finish_rows(den, jnp.add, jnp.sum)
        col_max = new_max

    wl = wl_ref[...]
    lam = (jnp.exp(jnp.sum(wl[0:1] * wl[1:2], axis=-1, keepdims=True))
           - jnp.exp(jnp.sum(wl[2:3] * wl[3:4], axis=-1, keepdims=True)) + li)
    for hd in range(n_heads):
        for sub in range(n_sub):
            ot = results[(hd, sub, 0)] - lam * results[(hd, sub, 1)]
            y = (_rms(ot, 0) * g_ref[...]) * (1.0 - li)
            o_ref[0, sub * QUERY_BLOCK:(sub + 1) * QUERY_BLOCK,
                  hd * HEAD_W:(hd + 1) * HEAD_W] = y.T.astype(BF16)


def _attention(qt, k, vt, cache, w_lambda_l, g_subln_l, li, n_seq, seq_len, n_heads, n_sub):
    G, _, _, T = qt.shape
    per_g = T // seq_len
    tq = n_sub * QUERY_BLOCK
    nq = seq_len // tq
    has_cache = cache is not None
    n_keys = seq_len

    def gi(b):
        return b // per_g

    def si(b):
        return b % per_g

    hw = n_heads * HEAD_W
    in_specs = [pl.BlockSpec((1, 2 * n_heads, HEAD_W, tq), lambda b, h, i: (gi(b), h, 0, si(b) * nq + i)),
                pl.BlockSpec((1, seq_len, hw), lambda b, h, i: (gi(b), si(b), h)),
                pl.BlockSpec((1, hw, seq_len), lambda b, h, i: (gi(b), h, si(b)))]
    args = [qt, k, vt]
    if has_cache:
        assert n_heads == 1
        ck, cv, layer = cache
        past = ck.shape[2]
        n_keys = seq_len + past
        in_specs += [pl.BlockSpec((1, None, past, HEAD_W), lambda b, h, i: (b, layer, 0, h))] * 2
        args += [ck, cv]
    scratch = [pltpu.VMEM((n_keys, QUERY_BLOCK), F32)] * 2
    if has_cache:
        scratch += [pltpu.VMEM((n_keys, HEAD_W), BF16), pltpu.VMEM((HEAD_W, n_keys), BF16)]
    in_specs += [pl.BlockSpec((4, HEAD_DIM), lambda b, h, i: (0, 0)),
                 pl.BlockSpec((HEAD_W, 1), lambda b, h, i: (0, 0))]
    args += [w_lambda_l, g_subln_l.reshape(HEAD_W, 1)]
    return pl.pallas_call(
        functools.partial(_attn_kernel, has_cache=has_cache, li=li, n_heads=n_heads,
                          n_sub=n_sub, n_keys=n_keys),
        grid=(n_seq, N_HEADS // n_heads, nq),
        in_specs=in_specs,
        out_specs=pl.BlockSpec((1, tq, hw), lambda b, h, i: (gi(b), si(b) * nq + i, h)),
        out_shape=jax.ShapeDtypeStruct((G, T, ATTN_W), BF16),
        scratch_shapes=scratch,
        compiler_params=_cparams("arbitrary", "arbitrary", "arbitrary"),
        name="attn_latent" if has_cache else "attn_ctx",
    )(*args)


def _scan_chunk(a, b, row, reverse):
    n = a.shape[0]
    d = 1
    while d < n:
        if reverse:
            shift, valid = n - d, row < n - d
        else:
            shift, valid = d, row >= d
        a_prev = jnp.where(valid, pltpu.roll(a, shift, 0), 1.0)
        b_prev = jnp.where(valid, pltpu.roll(b, shift, 0), 0.0)
        b = a * b_prev + b
        a = a * a_prev
        d *= 2
    return a, b


def _lru_kernel(xr_ref, gr_ref, cw_ref, cb_ref, w_ref, bias_ref, lam_ref, h0_ref,
                out_ref, hfin_ref, xpad, hf_scr, ab_scr, bb_scr, *, seq_len, tc):
    nch = seq_len // tc
    zeros_halo = jnp.zeros((CONV_HALO, LRU_W), F32)
    xpad[0:CONV_HALO] = zeros_halo
    xpad[seq_len + CONV_HALO:seq_len + 2 * CONV_HALO] = zeros_halo
    xpad[CONV_HALO:seq_len + CONV_HALO] = xr_ref[0]
    lam_p = lam_ref[...]
    neg = -lam_p
    softplus = jnp.maximum(neg, 0.0) + jnp.log1p(jnp.exp(-jnp.abs(neg)))
    coef = -RG_C * softplus
    cw = cw_ref[...]
    row = lax.broadcasted_iota(jnp.int32, (tc, LRU_W), 0)
    h0 = h0_ref[0]

    def decay_and_input(xc, z, direction):
        off = direction * 2 * LRU_W
        r = jax.nn.sigmoid(z[:, off:off + LRU_W])
        i = jax.nn.sigmoid(z[:, off + LRU_W:off + 2 * LRU_W])
        log_a = coef[direction:direction + 1] * r
        a = jnp.exp(log_a)
        b = jnp.sqrt(-jnp.tanh(log_a) * (1.0 + a * a)) * (i * xc)
        return a, b

    def forward_chunk(c, carry):
        start = pl.multiple_of(c * tc, tc)
        win = xpad[pl.ds(start, tc + 2 * CONV_HALO), :]
        lo = CONV_HALO - CONV_W // 2
        xc = win[lo:lo + tc] * cw[0:1]
        for j in range(1, CONV_W):
            xc = xc + win[lo + j:lo + j + tc] * cw[j:j + 1]
        xc = xc + cb_ref[...]
        z = jnp.dot(xc.astype(BF16), w_ref[...], preferred_element_type=F32) + bias_ref[...]
        a_f, b_f = decay_and_input(xc, z, 0)
        a_cum, h_loc = _scan_chunk(a_f, b_f, row, reverse=False)
        hf = a_cum * carry + h_loc
        hf_scr[pl.ds(start, tc), :] = hf
        a_b, b_b = decay_and_input(xc, z, 1)
        ab_scr[pl.ds(start, tc), :] = a_b
        bb_scr[pl.ds(start, tc), :] = b_b
        return hf[tc - 1:tc]

    def backward_chunk(j, carry):
        c = nch - 1 - j
        start = pl.multiple_of(c * tc, tc)
        a_cum, h_loc = _scan_chunk(ab_scr[pl.ds(start, tc), :], bb_scr[pl.ds(start, tc), :],
                                   row, reverse=True)
        hb = a_cum * carry + h_loc
        gate = jax.nn.gelu(gr_ref[0, pl.ds(start, tc), :], approximate=True)
        out_ref[0, pl.ds(start, tc), :] = ((hf_scr[pl.ds(start, tc), :] + hb) * gate).astype(BF16)
        return hb[0:1]

    if nch == 1:
        hf_fin = forward_chunk(0, h0[0:1])
        hb_fin = backward_chunk(0, h0[1:2])
    else:
        hf_fin = lax.fori_loop(0, nch, forward_chunk, h0[0:1])
        hb_fin = lax.fori_loop(0, nch, backward_chunk, h0[1:2])
    hfin_ref[0, 0:1] = hf_fin
    hfin_ref[0, 1:2] = hb_fin


def _rg_lru(rest, conv_w, conv_b, w_gates, b_gates, lam_p, h0):
    B, L, _ = rest.shape
    tc = min(L, 256)
    return pl.pallas_call(
        functools.partial(_lru_kernel, seq_len=L, tc=tc),
        grid=(B,),
        in_specs=[pl.BlockSpec((1, L, LRU_W), lambda b: (b, 0, 0)),
                  pl.BlockSpec((1, L, LRU_W), lambda b: (b, 0, 1)),
                  pl.BlockSpec((CONV_W, LRU_W), lambda b: (0, 0)),
                  pl.BlockSpec((1, LRU_W), lambda b: (0, 0)),
                  pl.BlockSpec((LRU_W, 4 * LRU_W), lambda b: (0, 0)),
                  pl.BlockSpec((1, 4 * LRU_W), lambda b: (0, 0)),
                  pl.BlockSpec((2, LRU_W), lambda b: (0, 0)),
                  pl.BlockSpec((1, 2, LRU_W), lambda b: (b, 0, 0))],
        out_specs=[pl.BlockSpec((1, L, LRU_W), lambda b: (b, 0, 0)),
                   pl.BlockSpec((1, 2, LRU_W), lambda b: (b, 0, 0))],
        out_shape=[jax.ShapeDtypeStruct((B, L, LRU_W), BF16),
                   jax.ShapeDtypeStruct((B, 2, LRU_W), F32)],
        scratch_shapes=[pltpu.VMEM((L + 2 * CONV_HALO, LRU_W), F32),
                        pltpu.VMEM((L, LRU_W), F32),
                        pltpu.VMEM((L, LRU_W), F32),
                        pltpu.VMEM((L, LRU_W), F32)],
        compiler_params=_cparams("arbitrary"),
        name="rg_lru",
    )(rest, rest, conv_w, conv_b.reshape(1, LRU_W), w_gates, b_gates, lam_p, h0)


def _chan_dft_kernel(x_ref, w_ref, yc_ref, ys_ref):
    y = jnp.dot(x_ref[0], w_ref[...], preferred_element_type=F32)
    yc_ref[...] = y[:, :FOURIER_W].astype(BF16)
    ys_ref[...] = y[:, FOURIER_W:].astype(BF16)


def _seq_dft_kernel(fc_ref, fs_ref, yc_ref, ys_ref, o_ref, *, scale):
    acc = jnp.dot(fc_ref[...], yc_ref[...], preferred_element_type=F32)
    acc = acc - jnp.dot(fs_ref[...], ys_ref[...], preferred_element_type=F32)
    o_ref[...] = (acc * scale).astype(BF16)


def _fourier_direct(xf, chan_dft, seq_cos, seq_sin):
    B, L, _ = xf.shape
    y_shape = jax.ShapeDtypeStruct((L, B * FOURIER_W), BF16)
    yc, ys = pl.pallas_call(
        _chan_dft_kernel,
        grid=(B,),
        in_specs=[pl.BlockSpec((1, L, FOURIER_W), lambda b: (b, 0, 0)),
                  pl.BlockSpec((FOURIER_W, 2 * FOURIER_W), lambda b: (0, 0))],
        out_specs=[pl.BlockSpec((L, FOURIER_W), lambda b: (0, b))] * 2,
        out_shape=[y_shape, y_shape],
        compiler_params=_cparams("arbitrary"),
        name="fourier_channels",
    )(xf, chan_dft)
    return pl.pallas_call(
        functools.partial(_seq_dft_kernel, scale=(L * FOURIER_GW) ** -0.5),
        grid=(B,),
        in_specs=[pl.BlockSpec((L, L), lambda b: (0, 0)),
                  pl.BlockSpec((L, L), lambda b: (0, 0)),
                  pl.BlockSpec((L, FOURIER_W), lambda b: (0, b)),
                  pl.BlockSpec((L, FOURIER_W), lambda b: (0, b))],
        out_specs=pl.BlockSpec((L, FOURIER_W), lambda b: (b, 0)),
        out_shape=jax.ShapeDtypeStruct((B * L, FOURIER_W), BF16),
        compiler_params=_cparams("arbitrary"),
        name="fourier_sequence",
    )(seq_cos, seq_sin, yc, ys)


FFT_RADIX = 64
FFT_COLS = 2048
FFT_K1_PER_STEP = 8


def _fft_stage1_kernel(x_ref, f_ref, tc_ref, ts_ref, o_ref):
    a = jnp.dot(f_ref[...], x_ref[0], preferred_element_type=F32)
    a_re, a_im = a[:FFT_RADIX], a[FFT_RADIX:]
    tc, ts = tc_ref[...], ts_ref[...]
    o_ref[0, 0] = (a_re * tc + a_im * ts).astype(BF16)
    o_ref[0, 1] = (a_im * tc - a_re * ts).astype(BF16)


def _fft_stage2_kernel(b_ref, f_ref, w_ref, o_ref, *, scale):
    zs = []
    for j in range(FFT_K1_PER_STEP):
        bj = jnp.concatenate([b_ref[0, 0, j], b_ref[0, 1, j]], axis=0)
        z = jnp.dot(f_ref[...], bj, preferred_element_type=F32)
        zs.append(jnp.concatenate([z[:FFT_RADIX], z[FFT_RADIX:]], axis=1))
    z_all = jnp.concatenate(zs, axis=0).astype(BF16)
    out = jnp.dot(z_all, w_ref[...], preferred_element_type=F32) * scale
    o_ref[0] = out.reshape(FFT_K1_PER_STEP, FFT_RADIX, FOURIER_W).astype(BF16)


def _fourier_fft(xf, stage1_dft, twiddle_cos, twiddle_sin, stage2_dft, chan_dft_stacked):
    B, L, _ = xf.shape
    R = FFT_RADIX
    assert L == R * R
    cols = R * FOURIER_W
    x2 = xf.reshape(B, R, cols)
    mid = pl.pallas_call(
        _fft_stage1_kernel,
        grid=(cols // FFT_COLS, B),
        in_specs=[pl.BlockSpec((1, R, FFT_COLS), lambda j, b: (b, 0, j)),
                  pl.BlockSpec((2 * R, R), lambda j, b: (0, 0)),
                  pl.BlockSpec((R, FFT_COLS), lambda j, b: (0, j)),
                  pl.BlockSpec((R, FFT_COLS), lambda j, b: (0, j))],
        out_specs=pl.BlockSpec((1, 2, R, FFT_COLS), lambda j, b: (b, 0, 0, j)),
        out_shape=jax.ShapeDtypeStruct((B, 2, R, cols), BF16),
        compiler_params=_cparams("arbitrary", "arbitrary"),
        name="fft_stage1",
    )(x2, stage1_dft, twiddle_cos, twiddle_sin)
    mid = mid.reshape(B, 2, R, R, FOURIER_W)
    out = pl.pallas_call(
        functools.partial(_fft_stage2_kernel, scale=(L * FOURIER_GW) ** -0.5),
        grid=(B, R // FFT_K1_PER_STEP),
        in_specs=[pl.BlockSpec((1, 2, FFT_K1_PER_STEP, R, FOURIER_W), lambda b, i: (b, 0, i, 0, 0)),
                  pl.BlockSpec((2 * R, 2 * R), lambda b, i: (0, 0)),
                  pl.BlockSpec((2 * FOURIER_W, FOURIER_W), lambda b, i: (0, 0))],
        out_specs=pl.BlockSpec((1, FFT_K1_PER_STEP, R, FOURIER_W), lambda b, i: (b, i, 0, 0)),
        out_shape=jax.ShapeDtypeStruct((B, R, R, FOURIER_W), BF16),
        compiler_params=_cparams("arbitrary", "arbitrary"),
        name="fft_stage2",
    )(mid, stage2_dft, chan_dft_stacked)
    return out.transpose(0, 2, 1, 3).reshape(B, L, FOURIER_W)


POST_TILE = 512
POST_ROW_GROUP = 256
def _post_kernel(x_ref, attn_ref, lru_ref, four_ref, mod_ref, gmix_ref, gpre_ref, gffn_ref,
                 wo_ref, wgu_ref, wd_ref, o_ref):
    mod = mod_ref[0]
    tm = x_ref.shape[1]
    groups = [slice(r, r + POST_ROW_GROUP) for r in range(0, tm, POST_ROW_GROUP)]

    def mix_out(rows):
        o = jnp.dot(attn_ref[0, rows], wo_ref[0:ATTN_W], preferred_element_type=F32)
        o = o + jnp.dot(lru_ref[0, rows], wo_ref[ATTN_W:ATTN_W + LRU_W], preferred_element_type=F32)
        return o + jnp.dot(four_ref[0, rows], wo_ref[ATTN_W + LRU_W:MIX_W], preferred_element_type=F32)

    def residual_and_prenorm(rows, o):
        x1 = x_ref[0, rows] + mod[2:3] * (_rms(o, -1) * gmix_ref[...])
        h = (_rms(x1, -1) * gpre_ref[...]) * (1.0 + mod[4:5]) + mod[3:4]
        return x1, h.astype(BF16)

    def gate_up(hb):
        gt = jnp.dot(hb, wgu_ref[:, :FFN_HIDDEN], preferred_element_type=F32)
        up = jnp.dot(hb, wgu_ref[:, FFN_HIDDEN:], preferred_element_type=F32)
        return ((gt * jax.nn.sigmoid(gt)) * up).astype(BF16)

    mixed = [mix_out(rows) for rows in groups]
    normed = [residual_and_prenorm(rows, o) for rows, o in zip(groups, mixed)]
    acts = [gate_up(hb) for _, hb in normed]
    downs = [jnp.dot(act, wd_ref[...], preferred_element_type=F32) for act in acts]
    for rows, (x1, _), f in zip(groups, normed, downs):
        o_ref[0, rows] = x1 + mod[5:6] * (_rms(f, -1) * gffn_ref[...])


def _post(x, attn, lru, four, mod, goff, g_post_mix, g_pre_ffn, g_post_ffn,
          w_out, w_gate_up, w_down, tm):
    G, T, _ = x.shape
    vec = pl.BlockSpec((1, D_MODEL), lambda g, i: (0, 0))
    return pl.pallas_call(
        _post_kernel,
        grid=(G, T // tm),
        in_specs=[pl.BlockSpec((1, tm, D_MODEL), lambda g, i: (g, i, 0)),
                  pl.BlockSpec((1, tm, ATTN_W), lambda g, i: (g, i, 0)),
                  pl.BlockSpec((1, tm, LRU_W), lambda g, i: (g, i, 0)),
                  pl.BlockSpec((1, tm, FOURIER_W), lambda g, i: (g, i, 0)),
                  pl.BlockSpec((1, N_MOD, D_MODEL), lambda g, i: (g + goff, 0, 0)),
                  vec, vec, vec,
                  _resident((MIX_W, D_MODEL), lambda g, i: (0, 0)),
                  _resident((D_MODEL, 2 * FFN_HIDDEN), lambda g, i: (0, 0)),
                  _resident((FFN_HIDDEN, D_MODEL), lambda g, i: (0, 0))],
        out_specs=pl.BlockSpec((1, tm, D_MODEL), lambda g, i: (g, i, 0)),
        out_shape=jax.ShapeDtypeStruct((G, T, D_MODEL), F32),
        compiler_params=_cparams("arbitrary", "arbitrary"),
        name="post",
    )(x, attn, lru, four, mod, g_post_mix, g_pre_ffn, g_post_ffn, w_out, w_gate_up, w_down)


def _rope_tables(n_tokens):
    rows = n_tokens // GRID_W
    row = jnp.repeat(jnp.arange(rows, dtype=F32), GRID_W)
    col = jnp.tile(jnp.arange(GRID_W, dtype=F32), rows)
    n = HEAD_DIM // 4
    inv = ROPE_BASE ** (-jnp.arange(n, dtype=F32) / n)
    ang_r, ang_c = row[:, None] * inv, col[:, None] * inv
    zero = jnp.zeros_like(ang_r)
    cos64 = jnp.concatenate([jnp.cos(ang_r)] * 2 + [jnp.cos(ang_c)] * 2, axis=-1)
    sin_lo = jnp.concatenate([-jnp.sin(ang_r), zero, -jnp.sin(ang_c), zero], axis=-1)
    sin_hi = jnp.concatenate([zero, jnp.sin(ang_r), zero, jnp.sin(ang_c)], axis=-1)
    return tuple(jnp.tile(t, (1, 2)) for t in (cos64, sin_lo, sin_hi))


def _seq_dft_tables(n):
    r = 1 << (n.bit_length() // 2)
    l = jnp.arange(n, dtype=jnp.int32)

    def cos_sin(rows, period):
        k = jnp.arange(rows, dtype=jnp.int32)
        ang = ((k[:, None] * l[None, :]) % period).astype(F32) * (2.0 * math.pi / period)
        return jnp.cos(ang), jnp.sin(ang)

    ca, sa = cos_sin(n // r, n // r)
    cb, sb = cos_sin(r, n)
    cos = ca[:, None, :] * cb[None] - sa[:, None, :] * sb[None]
    sin = sa[:, None, :] * cb[None] + ca[:, None, :] * sb[None]
    return cos.reshape(n, n).astype(BF16), sin.reshape(n, n).astype(BF16)


def _chan_dft_table():
    k = np.arange(FOURIER_GW)
    ang = 2.0 * np.pi * ((k[:, None] * k[None, :]) % FOURIER_GW) / FOURIER_GW
    eye = np.eye(FOURIER_GROUPS)
    table = np.concatenate([np.kron(eye, np.cos(ang)), np.kron(eye, np.sin(ang))], axis=1)
    return jnp.asarray(table, dtype=F32).astype(BF16)


def _fft_tables(n):
    r = FFT_RADIX
    k = np.arange(r)
    ang = 2.0 * np.pi * ((k[:, None] * k[None, :]) % r) / r
    cm, sm = np.cos(ang), np.sin(ang)
    stage1 = np.concatenate([cm, -sm], axis=0)
    stage2 = np.block([[cm, sm], [-sm, cm]])
    kj = jnp.arange(r, dtype=jnp.int32)
    tw = (kj[:, None] * kj[None, :]).astype(F32) * (2.0 * math.pi / n)
    expand = lambda t: jnp.repeat(t, FOURIER_W, axis=1)
    return (jnp.asarray(stage1, dtype=F32).astype(BF16), expand(jnp.cos(tw)), expand(jnp.sin(tw)),
            jnp.asarray(stage2, dtype=F32).astype(BF16))


def _block_diag(w):
    n, bw, _ = w.shape
    eye = jnp.eye(n, dtype=w.dtype)
    return (eye[:, None, :, None] * w[:, :, None, :]).reshape(n * bw, n * bw)


def kernel(x_prompt, x_sample, cache_k, cache_v, state_lru, c, c_ctx, w_mod, b_mod, g_pre_mix, g_post_mix, g_pre_ffn, g_post_ffn, w_in, w_out, w_lambda, g_subln, conv_w, conv_b, lru_wa, lru_ba, lru_wx, lru_bx, lru_lambda, w_gate_up, w_down):
    n_ctx, ctx_len, _ = x_prompt.shape
    n_lat, lat_len, _ = x_sample.shape
    past = cache_k.shape[2]

    cond = jnp.concatenate(
        [c_ctx[None, :], c, jnp.zeros((COND_ROWS - 1 - n_lat, D_MODEL), F32)], axis=0)
    mod_all = _modulation(cond, w_mod, b_mod).reshape(DEPTH, COND_ROWS, N_MOD, D_MODEL)

    rope_tabs = _rope_tables(lat_len)
    chan_dft = _chan_dft_table()
    chan_dft_stacked = jnp.concatenate([chan_dft[:, :FOURIER_W], chan_dft[:, FOURIER_W:]], axis=0)
    dft_ctx = _seq_dft_tables(ctx_len)
    fft_lat = _fft_tables(lat_len)
    ck = cache_k.reshape(n_lat, DEPTH, past, ATTN_W)
    cv = cache_v.reshape(n_lat, DEPTH, past, ATTN_W)
    h0_ctx = jnp.zeros((n_ctx, 2, LRU_W), F32)

    xp = x_prompt.reshape(1, n_ctx * ctx_len, D_MODEL)
    xs = x_sample
    new_k, new_v, new_h = [], [], []
    for l in range(DEPTH):
        li = _lambda_init(l)
        mod = mod_all[l]
        w_in_l = w_in[l].astype(BF16)
        w_out_l = w_out[l].astype(BF16)
        w_gu_l = w_gate_up[l].astype(BF16)
        w_down_l = w_down[l].astype(BF16)
        w_gates = jnp.concatenate(
            [_block_diag(lru_wa[l, 0]), _block_diag(lru_wx[l, 0]),
             _block_diag(lru_wa[l, 1]), _block_diag(lru_wx[l, 1])], axis=1).astype(BF16)
        b_gates = jnp.concatenate(
            [lru_ba[l, 0], lru_bx[l, 0], lru_ba[l, 1], lru_bx[l, 1]])[None, :]
        vecs = [g[l][None, :] for g in (g_post_mix, g_pre_ffn, g_post_ffn)]

        qt, k, vt, rest, xf, k_f32, v_f32 = _inproj(
            xp, mod, 0, g_pre_mix[l][None, :], w_in_l, None, emit_kv=True)
        new_k.append(k_f32.reshape(n_ctx, ctx_len, N_HEADS, 2, HEAD_DIM))
        new_v.append(v_f32.reshape(n_ctx, ctx_len, N_HEADS, HEAD_W))
        attn = _attention(qt, k, vt, None, w_lambda[l], g_subln[l], li,
                          n_seq=n_ctx, seq_len=ctx_len, n_heads=N_HEADS,
                          n_sub=ctx_len // QUERY_BLOCK)
        rest_seq = rest.reshape(n_ctx, ctx_len, REST_W)
        lru, h_fin = _rg_lru(rest_seq, conv_w[l], conv_b[l], w_gates, b_gates,
                             lru_lambda[l], h0_ctx)
        new_h.append(h_fin)
        four = _fourier_direct(xf.reshape(n_ctx, ctx_len, FOURIER_W), chan_dft, *dft_ctx)
        xp = _post(xp, attn, lru.reshape(1, n_ctx * ctx_len, LRU_W),
                   four.reshape(1, n_ctx * ctx_len, FOURIER_W), mod, 0,
                   *vecs, w_out_l, w_gu_l, w_down_l, tm=POST_TILE)

        qt, k, vt, rest, xf = _inproj(
            xs, mod, 1, g_pre_mix[l][None, :], w_in_l, rope_tabs, emit_kv=False)
        attn = _attention(qt, k, vt, (ck, cv, l), w_lambda[l], g_subln[l], li,
                          n_seq=n_lat, seq_len=lat_len, n_heads=1, n_sub=4)
        lru, _ = _rg_lru(rest, conv_w[l], conv_b[l], w_gates, b_gates,
                         lru_lambda[l], state_lru[:, l])
        four = _fourier_fft(xf, fft_lat[0], fft_lat[1], fft_lat[2], fft_lat[3], chan_dft_stacked)
        xs = _post(xs, attn, lru, four, mod, 1,
                   *vecs, w_out_l, w_gu_l, w_down_l, tm=POST_TILE)

    return (xp.reshape(n_ctx, ctx_len, D_MODEL), xs,
            jnp.stack(new_k, axis=1), jnp.stack(new_v, axis=1), jnp.stack(new_h, axis=1))
```

```python
import functools
import math

import jax
import jax.numpy as jnp
import numpy as np
from jax import lax
from jax.experimental import pallas as pl
from jax.experimental.pallas import tpu as pltpu

D_MODEL = 1024
DEPTH = 2
GRID_W = 64
HEAD_DIM = 64
N_HEADS = 4
HEAD_W = 2 * HEAD_DIM
ATTN_W = N_HEADS * HEAD_W
LRU_W = D_MODEL // 4
LRU_BLOCKS = 4
CONV_W = 4
FOURIER_W = D_MODEL // 4
FOURIER_GROUPS = 4
FOURIER_GW = FOURIER_W // FOURIER_GROUPS
MIX_W = ATTN_W + LRU_W + FOURIER_W
REST_W = 2 * LRU_W
IN_W = 3 * ATTN_W + REST_W + FOURIER_W
FFN_HIDDEN = -(-8 * D_MODEL // (3 * 256)) * 256
ROPE_BASE = 10000.0
RG_C = 8.0
EPS = 1e-6
N_MOD = 6
COND_ROWS = 8

F32 = jnp.float32
BF16 = jnp.bfloat16
V7X_VMEM_LIMIT_BYTES = 56 * 1024 * 1024
CONV_HALO = 8


def _cparams(*semantics, flags=None):
    return pltpu.CompilerParams(dimension_semantics=semantics,
                                vmem_limit_bytes=V7X_VMEM_LIMIT_BYTES, flags=flags)


def _resident(shape, index_map):
    return pl.BlockSpec(shape, index_map, pipeline_mode=pl.Buffered(1))


def _lambda_init(l):
    return 0.8 - 0.6 * math.exp(-0.3 * l)


def _rms(x, axis):
    return x * lax.rsqrt(jnp.mean(x * x, axis=axis, keepdims=True) + EPS)


def _mod_kernel(cond_ref, w_ref, b_ref, o_ref):
    s = cond_ref[...]
    s = s * jax.nn.sigmoid(s)
    o_ref[0] = jnp.dot(s, w_ref[0], precision=lax.Precision.HIGHEST,
                       preferred_element_type=F32) + b_ref[0]


def _modulation(cond, w_mod, b_mod):
    tn = 1536
    n_out = N_MOD * D_MODEL
    return pl.pallas_call(
        _mod_kernel,
        grid=(DEPTH, n_out // tn),
        in_specs=[pl.BlockSpec((COND_ROWS, D_MODEL), lambda l, j: (0, 0)),
                  pl.BlockSpec((1, D_MODEL, tn), lambda l, j: (l, 0, j)),
                  pl.BlockSpec((1, 1, tn), lambda l, j: (l, 0, j))],
        out_specs=pl.BlockSpec((1, COND_ROWS, tn), lambda l, j: (l, 0, j)),
        out_shape=jax.ShapeDtypeStruct((DEPTH, COND_ROWS, n_out), F32),
        compiler_params=_cparams("arbitrary", "arbitrary"),
        name="modulation",
    )(cond, w_mod, b_mod.reshape(DEPTH, 1, n_out))


def _inproj_kernel(*refs, rope, emit_kv, tm):
    x_ref, mod_ref, g_ref, w_ref = refs[:4]
    refs = refs[4:]
    if rope:
        cos_ref, sin_lo_ref, sin_hi_ref = refs[:3]
        refs = refs[3:]
    qt_ref, k_ref, vt_ref, rest_ref, xf_ref = refs[:5]
    x = x_ref[0]
    mod = mod_ref[0]
    h = (_rms(x, -1) * g_ref[...]) * (1.0 + mod[1:2]) + mod[0:1]
    proj = jnp.dot(h.astype(BF16), w_ref[...], preferred_element_type=F32)
    q = proj[:, :ATTN_W]
    k = proj[:, ATTN_W:2 * ATTN_W]
    v = proj[:, 2 * ATTN_W:3 * ATTN_W]
    if emit_kv:
        kf_ref, vf_ref = refs[5:7]
        kf_ref[...] = k
        vf_ref[...] = v

    def rotary(t):
        return (t * cos_ref[...] + pltpu.roll(t, HEAD_W - 16, 1) * sin_lo_ref[...]
                + pltpu.roll(t, 16, 1) * sin_hi_ref[...])

    row = lax.broadcasted_iota(jnp.int32, (HEAD_W, tm), 0)
    scale = HEAD_DIM ** -0.5 * math.log2(math.e)
    for hd in range(N_HEADS):
        qh = q[:, hd * HEAD_W:(hd + 1) * HEAD_W]
        kh = k[:, hd * HEAD_W:(hd + 1) * HEAD_W]
        if rope:
            qh = rotary(qh)
            kh = rotary(kh)
        qht = (qh * scale).T
        qt_ref[0, 2 * hd] = jnp.where(row < HEAD_DIM, qht, 0.0).astype(BF16)
        qt_ref[0, 2 * hd + 1] = jnp.where(row >= HEAD_DIM, qht, 0.0).astype(BF16)
        k_ref[0, :, hd * HEAD_W:(hd + 1) * HEAD_W] = kh.astype(BF16)
    vt_ref[0] = v.T.astype(BF16)
    rest_ref[0] = proj[:, 3 * ATTN_W:3 * ATTN_W + REST_W]
    xf_ref[0] = proj[:, 3 * ATTN_W + REST_W:].astype(BF16)


def _inproj(x, mod, goff, g_pre, w_in, rope_tabs, emit_kv):
    G, T, _ = x.shape
    tm = 512
    rope = rope_tabs is not None
    in_specs = [pl.BlockSpec((1, tm, D_MODEL), lambda g, i: (g, i, 0)),
                pl.BlockSpec((1, N_MOD, D_MODEL), lambda g, i: (g + goff, 0, 0)),
                pl.BlockSpec((1, D_MODEL), lambda g, i: (0, 0)),
                _resident((D_MODEL, IN_W), lambda g, i: (0, 0))]
    args = [x, mod, g_pre, w_in]
    if rope:
        in_specs += [pl.BlockSpec((tm, HEAD_W), lambda g, i: (i, 0))] * 3
        args += list(rope_tabs)
    out_specs = [pl.BlockSpec((1, 2 * N_HEADS, HEAD_W, tm), lambda g, i: (g, 0, 0, i)),
                 pl.BlockSpec((1, tm, ATTN_W), lambda g, i: (g, i, 0)),
                 pl.BlockSpec((1, ATTN_W, tm), lambda g, i: (g, 0, i)),
                 pl.BlockSpec((1, tm, REST_W), lambda g, i: (g, i, 0)),
                 pl.BlockSpec((1, tm, FOURIER_W), lambda g, i: (g, i, 0))]
    out_shape = [jax.ShapeDtypeStruct((G, 2 * N_HEADS, HEAD_W, T), BF16),
                 jax.ShapeDtypeStruct((G, T, ATTN_W), BF16),
                 jax.ShapeDtypeStruct((G, ATTN_W, T), BF16),
                 jax.ShapeDtypeStruct((G, T, REST_W), F32),
                 jax.ShapeDtypeStruct((G, T, FOURIER_W), BF16)]
    if emit_kv:
        assert G == 1
        out_specs += [pl.BlockSpec((tm, ATTN_W), lambda g, i: (i, 0))] * 2
        out_shape += [jax.ShapeDtypeStruct((T, ATTN_W), F32)] * 2
    return pl.pallas_call(
        functools.partial(_inproj_kernel, rope=rope, emit_kv=emit_kv, tm=tm),
        grid=(G, T // tm),
        in_specs=in_specs, out_specs=out_specs, out_shape=out_shape,
        compiler_params=_cparams("arbitrary", "arbitrary"),
        name="inproj_rope" if rope else "inproj_ctx",
    )(*args)


QUERY_BLOCK = 256
KEY_CHUNK = 256
SUBLANES = 8
SUM_ACCUMULATORS = 2


ROW_ACCUMULATORS = 2


def _accumulate_rows(accs, x, op):
    for idx, r in enumerate(range(0, x.shape[0], SUBLANES)):
        j = idx % len(accs)
        part = x[r:r + SUBLANES]
        accs[j] = part if accs[j] is None else op(accs[j], part)


def _finish_rows(accs, op, reduce):
    total = accs[0]
    for a in accs[1:]:
        total = op(total, a)
    return reduce(total, axis=0, keepdims=True)


def _attn_kernel(*refs, has_cache, li, n_heads, n_sub, n_keys):
    qt_ref, k_ref, vt_ref = refs[:3]
    refs = refs[3:]
    if has_cache:
        ck_ref, cv_ref = refs[:2]
        refs = refs[2:]
    wl_ref, g_ref, o_ref = refs[:3]
    s_bufs = refs[3:5]
    acc_bufs = refs[5:7]
    n_self = k_ref.shape[1]
    if has_cache:
        kall, vtall = refs[7:9]

        @pl.when(pl.program_id(2) == 0)
        def _():
            kall[0:n_self] = k_ref[0]
            kall[n_self:n_keys] = ck_ref[0].astype(BF16)
            vtall[:, 0:n_self] = vt_ref[0]
            vtall[:, n_self:n_keys] = cv_ref[0].T.astype(BF16)

    def k_chunk(hd, c):
        rows = slice(c * KEY_CHUNK, (c + 1) * KEY_CHUNK)
        return kall[rows, :] if has_cache else k_ref[0, rows, hd * HEAD_W:(hd + 1) * HEAD_W]

    def vt_chunk(hd, c):
        cols = slice(c * KEY_CHUNK, (c + 1) * KEY_CHUNK)
        return vtall[:, cols] if has_cache else vt_ref[0, hd * HEAD_W:(hd + 1) * HEAD_W, cols]

    units = [(hd, sub, m) for hd in range(n_heads) for sub in range(n_sub) for m in range(2)]
    n_chunks = n_keys // KEY_CHUNK
    results = {}
    col_max = None
    for t in range(len(units) + 1):
        if t < len(units):
            hd, sub, m = units[t]
            qt = qt_ref[0, 2 * hd + m, :, sub * QUERY_BLOCK:(sub + 1) * QUERY_BLOCK]
            new_max = [None] * ROW_ACCUMULATORS
        if t > 0:
            hd_p = units[t - 1][0]
            mx = _finish_rows(col_max, jnp.maximum, jnp.max)
            den = [None] * SUM_ACCUMULATORS
            acc_ref = acc_bufs[(t - 1) % 2]
        for c in range(n_chunks):
            rows = slice(c * KEY_CHUNK, (c + 1) * KEY_CHUNK)
            if t < len(units):
                s = jnp.dot(k_chunk(hd, c), qt, preferred_element_type=F32)
                s_bufs[t % 2][rows, :] = s
                _accumulate_rows(new_max, s, jnp.maximum)
            if t > 0:
                e = jnp.exp2(s_bufs[(t - 1) % 2][rows, :] - mx)
                _accumulate_rows(den, e, jnp.add)
                pv = jnp.dot(vt_chunk(hd_p, c), e.astype(BF16), preferred_element_type=F32)
                if c == 0:
                    acc_ref[...] = pv
                else:
                    acc_ref[...] += pv
        if t > 0:
            results[units[t - 1]] = acc_ref[...] / _finish_rows(den, jnp.add, jnp.sum)
        col_max = new_max

    wl = wl_ref[...]
    lam = (jnp.exp(jnp.sum(wl[0:1] * wl[1:2], axis=-1, keepdims=True))
           - jnp.exp(jnp.sum(wl[2:3] * wl[3:4], axis=-1, keepdims=True)) + li)
    for hd in range(n_heads):
        for sub in range(n_sub):
            ot = results[(hd, sub, 0)] - lam * results[(hd, sub, 1)]
            y = (_rms(ot, 0) * g_ref[...]) * (1.0 - li)
            o_ref[0, sub * QUERY_BLOCK:(sub + 1) * QUERY_BLOCK,
                  hd * HEAD_W:(hd + 1) * HEAD_W] = y.T.astype(BF16)


def _attention(qt, k, vt, cache, w_lambda_l, g_subln_l, li, n_seq, seq_len, n_heads, n_sub):
    G, _, _, T = qt.shape
    per_g = T // seq_len
    tq = n_sub * QUERY_BLOCK
    nq = seq_len // tq
    has_cache = cache is not None
    n_keys = seq_len

    def gi(b):
        return b // per_g

    def si(b):
        return b % per_g

    hw = n_heads * HEAD_W
    in_specs = [pl.BlockSpec((1, 2 * n_heads, HEAD_W, tq), lambda b, h, i: (gi(b), h, 0, si(b) * nq + i)),
                pl.BlockSpec((1, seq_len, hw), lambda b, h, i: (gi(b), si(b), h)),
                pl.BlockSpec((1, hw, seq_len), lambda b, h, i: (gi(b), h, si(b)))]
    args = [qt, k, vt]
    if has_cache:
        assert n_heads == 1
        ck, cv, layer = cache
        past = ck.shape[2]
        n_keys = seq_len + past
        in_specs += [pl.BlockSpec((1, None, past, HEAD_W), lambda b, h, i: (b, layer, 0, h))] * 2
        args += [ck, cv]
    scratch = [pltpu.VMEM((n_keys, QUERY_BLOCK), F32)] * 2
    scratch += [pltpu.VMEM((HEAD_W, QUERY_BLOCK), F32)] * 2
    if has_cache:
        scratch += [pltpu.VMEM((n_keys, HEAD_W), BF16), pltpu.VMEM((HEAD_W, n_keys), BF16)]
    in_specs += [pl.BlockSpec((4, HEAD_DIM), lambda b, h, i: (0, 0)),
                 pl.BlockSpec((HEAD_W, 1), lambda b, h, i: (0, 0))]
    args += [w_lambda_l, g_subln_l.reshape(HEAD_W, 1)]
    return pl.pallas_call(
        functools.partial(_attn_kernel, has_cache=has_cache, li=li, n_heads=n_heads,
                          n_sub=n_sub, n_keys=n_keys),
        grid=(n_seq, N_HEADS // n_heads, nq),
        in_specs=in_specs,
        out_specs=pl.BlockSpec((1, tq, hw), lambda b, h, i: (gi(b), si(b) * nq + i, h)),
        out_shape=jax.ShapeDtypeStruct((G, T, ATTN_W), BF16),
        scratch_shapes=scratch,
        compiler_params=_cparams("arbitrary", "arbitrary", "arbitrary"),
        name="attn_latent" if has_cache else "attn_ctx",
    )(*args)


def _scan_chunk(a, b, carry, reverse):
    n, w = a.shape
    groups = n // SUBLANES
    a = a.reshape(groups, SUBLANES, w)
    b = b.reshape(groups, SUBLANES, w)
    sub = lax.broadcasted_iota(jnp.int32, (groups, SUBLANES, w), 1)
    d = 1
    while d < SUBLANES:
        if reverse:
            shift, valid = SUBLANES - d, sub < SUBLANES - d
        else:
            shift, valid = d, sub >= d
        a_prev = jnp.where(valid, pltpu.roll(a, shift, 1), 1.0)
        b_prev = jnp.where(valid, pltpu.roll(b, shift, 1), 0.0)
        b = a * b_prev + b
        a = a * a_prev
        d *= 2
    edge = 0 if reverse else SUBLANES - 1
    hs = [None] * groups
    for g in (range(groups - 1, -1, -1) if reverse else range(groups)):
        hs[g] = a[g] * carry + b[g]
        carry = hs[g][edge:edge + 1]
    return jnp.concatenate(hs, axis=0), carry


def _lru_kernel(xr_ref, gr_ref, cw_ref, cb_ref, w_ref, bias_ref, lam_ref, h0_ref,
                out_ref, hfin_ref, xpad, hf_scr, ab_scr, bb_scr, *, seq_len, tc):
    nch = seq_len // tc
    zeros_halo = jnp.zeros((CONV_HALO, LRU_W), F32)
    xpad[0:CONV_HALO] = zeros_halo
    xpad[seq_len + CONV_HALO:seq_len + 2 * CONV_HALO] = zeros_halo
    xpad[CONV_HALO:seq_len + CONV_HALO] = xr_ref[0]
    lam_p = lam_ref[...]
    neg = -lam_p
    softplus = jnp.maximum(neg, 0.0) + jnp.log1p(jnp.exp(-jnp.abs(neg)))
    coef = -RG_C * softplus
    cw = cw_ref[...]
    h0 = h0_ref[0]

    def decay_and_input(xc, z, direction):
        off = direction * 2 * LRU_W
        r = jax.nn.sigmoid(z[:, off:off + LRU_W])
        i = jax.nn.sigmoid(z[:, off + LRU_W:off + 2 * LRU_W])
        log_a = coef[direction:direction + 1] * r
        a = jnp.exp(log_a)
        b = jnp.sqrt(-jnp.tanh(log_a) * (1.0 + a * a)) * (i * xc)
        return a, b

    def forward_chunk(c, carry):
        start = pl.multiple_of(c * tc, tc)
        win = xpad[pl.ds(start, tc + 2 * CONV_HALO), :]
        lo = CONV_HALO - CONV_W // 2
        xc = win[lo:lo + tc] * cw[0:1]
        for j in range(1, CONV_W):
            xc = xc + win[lo + j:lo + j + tc] * cw[j:j + 1]
        xc = xc + cb_ref[...]
        z = jnp.dot(xc.astype(BF16), w_ref[...], preferred_element_type=F32) + bias_ref[...]
        a_f, b_f = decay_and_input(xc, z, 0)
        hf, carry = _scan_chunk(a_f, b_f, carry, reverse=False)
        hf_scr[pl.ds(start, tc), :] = hf
        a_b, b_b = decay_and_input(xc, z, 1)
        ab_scr[pl.ds(start, tc), :] = a_b
        bb_scr[pl.ds(start, tc), :] = b_b
        return carry

    def backward_chunk(j, carry):
        c = nch - 1 - j
        start = pl.multiple_of(c * tc, tc)
        hb, carry = _scan_chunk(ab_scr[pl.ds(start, tc), :], bb_scr[pl.ds(start, tc), :],
                                carry, reverse=True)
        gate = jax.nn.gelu(gr_ref[0, pl.ds(start, tc), :], approximate=True)
        out_ref[0, pl.ds(start, tc), :] = ((hf_scr[pl.ds(start, tc), :] + hb) * gate).astype(BF16)
        return carry

    if nch == 1:
        hf_fin = forward_chunk(0, h0[0:1])
        hb_fin = backward_chunk(0, h0[1:2])
    else:
        hf_fin = lax.fori_loop(0, nch, forward_chunk, h0[0:1])
        hb_fin = lax.fori_loop(0, nch, backward_chunk, h0[1:2])
    hfin_ref[0, 0:1] = hf_fin
    hfin_ref[0, 1:2] = hb_fin


def _rg_lru(rest, conv_w, conv_b, w_gates, b_gates, lam_p, h0):
    B, L, _ = rest.shape
    tc = min(L, 256)
    return pl.pallas_call(
        functools.partial(_lru_kernel, seq_len=L, tc=tc),
        grid=(B,),
        in_specs=[pl.BlockSpec((1, L, LRU_W), lambda b: (b, 0, 0)),
                  pl.BlockSpec((1, L, LRU_W), lambda b: (b, 0, 1)),
                  pl.BlockSpec((CONV_W, LRU_W), lambda b: (0, 0)),
                  pl.BlockSpec((1, LRU_W), lambda b: (0, 0)),
                  pl.BlockSpec((LRU_W, 4 * LRU_W), lambda b: (0, 0)),
                  pl.BlockSpec((1, 4 * LRU_W), lambda b: (0, 0)),
                  pl.BlockSpec((2, LRU_W), lambda b: (0, 0)),
                  pl.BlockSpec((1, 2, LRU_W), lambda b: (b, 0, 0))],
        out_specs=[pl.BlockSpec((1, L, LRU_W), lambda b: (b, 0, 0)),
                   pl.BlockSpec((1, 2, LRU_W), lambda b: (b, 0, 0))],
        out_shape=[jax.ShapeDtypeStruct((B, L, LRU_W), BF16),
                   jax.ShapeDtypeStruct((B, 2, LRU_W), F32)],
        scratch_shapes=[pltpu.VMEM((L + 2 * CONV_HALO, LRU_W), F32),
                        pltpu.VMEM((L, LRU_W), F32),
                        pltpu.VMEM((L, LRU_W), F32),
                        pltpu.VMEM((L, LRU_W), F32)],
        compiler_params=_cparams("arbitrary"),
        name="rg_lru",
    )(rest, rest, conv_w, conv_b.reshape(1, LRU_W), w_gates, b_gates, lam_p, h0)


def _chan_dft_kernel(x_ref, w_ref, yc_ref, ys_ref):
    y = jnp.dot(x_ref[0], w_ref[...], preferred_element_type=F32)
    yc_ref[...] = y[:, :FOURIER_W].astype(BF16)
    ys_ref[...] = y[:, FOURIER_W:].astype(BF16)


def _seq_dft_kernel(fc_ref, fs_ref, yc_ref, ys_ref, o_ref, *, scale):
    acc = jnp.dot(fc_ref[...], yc_ref[...], preferred_element_type=F32)
    acc = acc - jnp.dot(fs_ref[...], ys_ref[...], preferred_element_type=F32)
    o_ref[...] = (acc * scale).astype(BF16)


def _fourier_direct(xf, chan_dft, seq_cos, seq_sin):
    B, L, _ = xf.shape
    y_shape = jax.ShapeDtypeStruct((L, B * FOURIER_W), BF16)
    yc, ys = pl.pallas_call(
        _chan_dft_kernel,
        grid=(B,),
        in_specs=[pl.BlockSpec((1, L, FOURIER_W), lambda b: (b, 0, 0)),
                  pl.BlockSpec((FOURIER_W, 2 * FOURIER_W), lambda b: (0, 0))],
        out_specs=[pl.BlockSpec((L, FOURIER_W), lambda b: (0, b))] * 2,
        out_shape=[y_shape, y_shape],
        compiler_params=_cparams("arbitrary"),
        name="fourier_channels",
    )(xf, chan_dft)
    return pl.pallas_call(
        functools.partial(_seq_dft_kernel, scale=(L * FOURIER_GW) ** -0.5),
        grid=(B,),
        in_specs=[pl.BlockSpec((L, L), lambda b: (0, 0)),
                  pl.BlockSpec((L, L), lambda b: (0, 0)),
                  pl.BlockSpec((L, FOURIER_W), lambda b: (0, b)),
                  pl.BlockSpec((L, FOURIER_W), lambda b: (0, b))],
        out_specs=pl.BlockSpec((L, FOURIER_W), lambda b: (b, 0)),
        out_shape=jax.ShapeDtypeStruct((B * L, FOURIER_W), BF16),
        compiler_params=_cparams("arbitrary"),
        name="fourier_sequence",
    )(seq_cos, seq_sin, yc, ys)


FFT_RADIX = 64
FFT_COLS = 2048
FFT_K1_PER_STEP = 8


def _fft_stage1_kernel(x_ref, f_ref, tc_ref, ts_ref, o_ref):
    a = jnp.dot(f_ref[...], x_ref[0], preferred_element_type=F32)
    a_re, a_im = a[:FFT_RADIX], a[FFT_RADIX:]
    tc, ts = tc_ref[...], ts_ref[...]
    o_ref[0, 0] = (a_re * tc + a_im * ts).astype(BF16)
    o_ref[0, 1] = (a_im * tc - a_re * ts).astype(BF16)


def _fft_stage2_kernel(b_ref, f_ref, w_ref, o_ref, *, scale):
    zs = []
    for j in range(FFT_K1_PER_STEP):
        bj = jnp.concatenate([b_ref[0, 0, j], b_ref[0, 1, j]], axis=0)
        z = jnp.dot(f_ref[...], bj, preferred_element_type=F32)
        zs.append(jnp.concatenate([z[:FFT_RADIX], z[FFT_RADIX:]], axis=1))
    z_all = jnp.concatenate(zs, axis=0).astype(BF16)
    out = jnp.dot(z_all, w_ref[...], preferred_element_type=F32) * scale
    o_ref[0] = out.reshape(FFT_K1_PER_STEP, FFT_RADIX, FOURIER_W).astype(BF16)


def _fourier_fft(xf, stage1_dft, twiddle_cos, twiddle_sin, stage2_dft, chan_dft_stacked):
    B, L, _ = xf.shape
    R = FFT_RADIX
    assert L == R * R
    cols = R * FOURIER_W
    x2 = xf.reshape(B, R, cols)
    mid = pl.pallas_call(
        _fft_stage1_kernel,
        grid=(cols // FFT_COLS, B),
        in_specs=[pl.BlockSpec((1, R, FFT_COLS), lambda j, b: (b, 0, j)),
                  pl.BlockSpec((2 * R, R), lambda j, b: (0, 0)),
                  pl.BlockSpec((R, FFT_COLS), lambda j, b: (0, j)),
                  pl.BlockSpec((R, FFT_COLS), lambda j, b: (0, j))],
        out_specs=pl.BlockSpec((1, 2, R, FFT_COLS), lambda j, b: (b, 0, 0, j)),
        out_shape=jax.ShapeDtypeStruct((B, 2, R, cols), BF16),
        compiler_params=_cparams("arbitrary", "arbitrary"),
        name="fft_stage1",
    )(x2, stage1_dft, twiddle_cos, twiddle_sin)
    mid = mid.reshape(B, 2, R, R, FOURIER_W)
    out = pl.pallas_call(
        functools.partial(_fft_stage2_kernel, scale=(L * FOURIER_GW) ** -0.5),
        grid=(B, R // FFT_K1_PER_STEP),
        in_specs=[pl.BlockSpec((1, 2, FFT_K1_PER_STEP, R, FOURIER_W), lambda b, i: (b, 0, i, 0, 0)),
                  pl.BlockSpec((2 * R, 2 * R), lambda b, i: (0, 0)),
                  pl.BlockSpec((2 * FOURIER_W, FOURIER_W), lambda b, i: (0, 0))],
        out_specs=pl.BlockSpec((1, FFT_K1_PER_STEP, R, FOURIER_W), lambda b, i: (b, i, 0, 0)),
        out_shape=jax.ShapeDtypeStruct((B, R, R, FOURIER_W), BF16),
        compiler_params=_cparams("arbitrary", "arbitrary"),
        name="fft_stage2",
    )(mid, stage2_dft, chan_dft_stacked)
    return out.transpose(0, 2, 1, 3).reshape(B, L, FOURIER_W)


POST_TILE = 512
POST_ROW_GROUP = 256
def _post_kernel(x_ref, attn_ref, lru_ref, four_ref, mod_ref, gmix_ref, gpre_ref, gffn_ref,
                 wo_ref, wgu_ref, wd_ref, o_ref):
    mod = mod_ref[0]
    tm = x_ref.shape[1]
    groups = [slice(r, r + POST_ROW_GROUP) for r in range(0, tm, POST_ROW_GROUP)]

    def mix_out(rows):
        o = jnp.dot(attn_ref[0, rows], wo_ref[0:ATTN_W], preferred_element_type=F32)
        o = o + jnp.dot(lru_ref[0, rows], wo_ref[ATTN_W:ATTN_W + LRU_W], preferred_element_type=F32)
        return o + jnp.dot(four_ref[0, rows], wo_ref[ATTN_W + LRU_W:MIX_W], preferred_element_type=F32)

    def residual_and_prenorm(rows, o):
        x1 = x_ref[0, rows] + mod[2:3] * (_rms(o, -1) * gmix_ref[...])
        h = (_rms(x1, -1) * gpre_ref[...]) * (1.0 + mod[4:5]) + mod[3:4]
        return x1, h.astype(BF16)

    def gate_up(hb):
        gt = jnp.dot(hb, wgu_ref[:, :FFN_HIDDEN], preferred_element_type=F32)
        up = jnp.dot(hb, wgu_ref[:, FFN_HIDDEN:], preferred_element_type=F32)
        return ((gt * jax.nn.sigmoid(gt)) * up).astype(BF16)

    mixed = [mix_out(rows) for rows in groups]
    normed = [residual_and_prenorm(rows, o) for rows, o in zip(groups, mixed)]
    acts = [gate_up(hb) for _, hb in normed]
    downs = [jnp.dot(act, wd_ref[...], preferred_element_type=F32) for act in acts]
    for rows, (x1, _), f in zip(groups, normed, downs):
        o_ref[0, rows] = x1 + mod[5:6] * (_rms(f, -1) * gffn_ref[...])


def _post(x, attn, lru, four, mod, goff, g_post_mix, g_pre_ffn, g_post_ffn,
          w_out, w_gate_up, w_down, tm):
    G, T, _ = x.shape
    vec = pl.BlockSpec((1, D_MODEL), lambda g, i: (0, 0))
    return pl.pallas_call(
        _post_kernel,
        grid=(G, T // tm),
        in_specs=[pl.BlockSpec((1, tm, D_MODEL), lambda g, i: (g, i, 0)),
                  pl.BlockSpec((1, tm, ATTN_W), lambda g, i: (g, i, 0)),
                  pl.BlockSpec((1, tm, LRU_W), lambda g, i: (g, i, 0)),
                  pl.BlockSpec((1, tm, FOURIER_W), lambda g, i: (g, i, 0)),
                  pl.BlockSpec((1, N_MOD, D_MODEL), lambda g, i: (g + goff, 0, 0)),
                  vec, vec, vec,
                  _resident((MIX_W, D_MODEL), lambda g, i: (0, 0)),
                  _resident((D_MODEL, 2 * FFN_HIDDEN), lambda g, i: (0, 0)),
                  _resident((FFN_HIDDEN, D_MODEL), lambda g, i: (0, 0))],
        out_specs=pl.BlockSpec((1, tm, D_MODEL), lambda g, i: (g, i, 0)),
        out_shape=jax.ShapeDtypeStruct((G, T, D_MODEL), F32),
        compiler_params=_cparams("arbitrary", "arbitrary"),
        name="post",
    )(x, attn, lru, four, mod, g_post_mix, g_pre_ffn, g_post_ffn, w_out, w_gate_up, w_down)


def _rope_tables(n_tokens):
    rows = n_tokens // GRID_W
    row = jnp.repeat(jnp.arange(rows, dtype=F32), GRID_W)
    col = jnp.tile(jnp.arange(GRID_W, dtype=F32), rows)
    n = HEAD_DIM // 4
    inv = ROPE_BASE ** (-jnp.arange(n, dtype=F32) / n)
    ang_r, ang_c = row[:, None] * inv, col[:, None] * inv
    zero = jnp.zeros_like(ang_r)
    cos64 = jnp.concatenate([jnp.cos(ang_r)] * 2 + [jnp.cos(ang_c)] * 2, axis=-1)
    sin_lo = jnp.concatenate([-jnp.sin(ang_r), zero, -jnp.sin(ang_c), zero], axis=-1)
    sin_hi = jnp.concatenate([zero, jnp.sin(ang_r), zero, jnp.sin(ang_c)], axis=-1)
    return tuple(jnp.tile(t, (1, 2)) for t in (cos64, sin_lo, sin_hi))


def _seq_dft_tables(n):
    r = 1 << (n.bit_length() // 2)
    l = jnp.arange(n, dtype=jnp.int32)

    def cos_sin(rows, period):
        k = jnp.arange(rows, dtype=jnp.int32)
        ang = ((k[:, None] * l[None, :]) % period).astype(F32) * (2.0 * math.pi / period)
        return jnp.cos(ang), jnp.sin(ang)

    ca, sa = cos_sin(n // r, n // r)
    cb, sb = cos_sin(r, n)
    cos = ca[:, None, :] * cb[None] - sa[:, None, :] * sb[None]
    sin = sa[:, None, :] * cb[None] + ca[:, None, :] * sb[None]
    return cos.reshape(n, n).astype(BF16), sin.reshape(n, n).astype(BF16)


def _chan_dft_table():
    k = np.arange(FOURIER_GW)
    ang = 2.0 * np.pi * ((k[:, None] * k[None, :]) % FOURIER_GW) / FOURIER_GW
    eye = np.eye(FOURIER_GROUPS)
    table = np.concatenate([np.kron(eye, np.cos(ang)), np.kron(eye, np.sin(ang))], axis=1)
    return jnp.asarray(table, dtype=F32).astype(BF16)


def _fft_tables(n):
    r = FFT_RADIX
    k = np.arange(r)
    ang = 2.0 * np.pi * ((k[:, None] * k[None, :]) % r) / r
    cm, sm = np.cos(ang), np.sin(ang)
    stage1 = np.concatenate([cm, -sm], axis=0)
    stage2 = np.block([[cm, sm], [-sm, cm]])
    kj = jnp.arange(r, dtype=jnp.int32)
    tw = (kj[:, None] * kj[None, :]).astype(F32) * (2.0 * math.pi / n)
    expand = lambda t: jnp.repeat(t, FOURIER_W, axis=1)
    return (jnp.asarray(stage1, dtype=F32).astype(BF16), expand(jnp.cos(tw)), expand(jnp.sin(tw)),
            jnp.asarray(stage2, dtype=F32).astype(BF16))


def _block_diag(w):
    n, bw, _ = w.shape
    eye = jnp.eye(n, dtype=w.dtype)
    return (eye[:, None, :, None] * w[:, :, None, :]).reshape(n * bw, n * bw)


def kernel(x_prompt, x_sample, cache_k, cache_v, state_lru, c, c_ctx, w_mod, b_mod, g_pre_mix, g_post_mix, g_pre_ffn, g_post_ffn, w_in, w_out, w_lambda, g_subln, conv_w, conv_b, lru_wa, lru_ba, lru_wx, lru_bx, lru_lambda, w_gate_up, w_down):
    n_ctx, ctx_len, _ = x_prompt.shape
    n_lat, lat_len, _ = x_sample.shape
    past = cache_k.shape[2]

    cond = jnp.concatenate(
        [c_ctx[None, :], c, jnp.zeros((COND_ROWS - 1 - n_lat, D_MODEL), F32)], axis=0)
    mod_all = _modulation(cond, w_mod, b_mod).reshape(DEPTH, COND_ROWS, N_MOD, D_MODEL)

    rope_tabs = _rope_tables(lat_len)
    chan_dft = _chan_dft_table()
    chan_dft_stacked = jnp.concatenate([chan_dft[:, :FOURIER_W], chan_dft[:, FOURIER_W:]], axis=0)
    dft_ctx = _seq_dft_tables(ctx_len)
    fft_lat = _fft_tables(lat_len)
    ck = cache_k.reshape(n_lat, DEPTH, past, ATTN_W)
    cv = cache_v.reshape(n_lat, DEPTH, past, ATTN_W)
    h0_ctx = jnp.zeros((n_ctx, 2, LRU_W), F32)

    xp = x_prompt.reshape(1, n_ctx * ctx_len, D_MODEL)
    xs = x_sample
    new_k, new_v, new_h = [], [], []
    for l in range(DEPTH):
        li = _lambda_init(l)
        mod = mod_all[l]
        w_in_l = w_in[l].astype(BF16)
        w_out_l = w_out[l].astype(BF16)
        w_gu_l = w_gate_up[l].astype(BF16)
        w_down_l = w_down[l].astype(BF16)
        w_gates = jnp.concatenate(
            [_block_diag(lru_wa[l, 0]), _block_diag(lru_wx[l, 0]),
             _block_diag(lru_wa[l, 1]), _block_diag(lru_wx[l, 1])], axis=1).astype(BF16)
        b_gates = jnp.concatenate(
            [lru_ba[l, 0], lru_bx[l, 0], lru_ba[l, 1], lru_bx[l, 1]])[None, :]
        vecs = [g[l][None, :] for g in (g_post_mix, g_pre_ffn, g_post_ffn)]

        qt, k, vt, rest, xf, k_f32, v_f32 = _inproj(
            xp, mod, 0, g_pre_mix[l][None, :], w_in_l, None, emit_kv=True)
        new_k.append(k_f32.reshape(n_ctx, ctx_len, N_HEADS, 2, HEAD_DIM))
        new_v.append(v_f32.reshape(n_ctx, ctx_len, N_HEADS, HEAD_W))
        attn = _attention(qt, k, vt, None, w_lambda[l], g_subln[l], li,
                          n_seq=n_ctx, seq_len=ctx_len, n_heads=N_HEADS,
                          n_sub=ctx_len // QUERY_BLOCK)
        rest_seq = rest.reshape(n_ctx, ctx_len, REST_W)
        lru, h_fin = _rg_lru(rest_seq, conv_w[l], conv_b[l], w_gates, b_gates,
                             lru_lambda[l], h0_ctx)
        new_h.append(h_fin)
        four = _fourier_direct(xf.reshape(n_ctx, ctx_len, FOURIER_W), chan_dft, *dft_ctx)
        xp = _post(xp, attn, lru.reshape(1, n_ctx * ctx_len, LRU_W),
                   four.reshape(1, n_ctx * ctx_len, FOURIER_W), mod, 0,
                   *vecs, w_out_l, w_gu_l, w_down_l, tm=POST_TILE)

        qt, k, vt, rest, xf = _inproj(
            xs, mod, 1, g_pre_mix[l][None, :], w_in_l, rope_tabs, emit_kv=False)
        attn = _attention(qt, k, vt, (ck, cv, l), w_lambda[l], g_subln[l], li,
                          n_seq=n_lat, seq_len=lat_len, n_heads=1, n_sub=4)
        lru, _ = _rg_lru(rest, conv_w[l], conv_b[l], w_gates, b_gates,
                         lru_lambda[l], state_lru[:, l])
        four = _fourier_fft(xf, fft_lat[0], fft_lat[1], fft_lat[2], fft_lat[3], chan_dft_stacked)
        xs = _post(xs, attn, lru, four, mod, 1,
                   *vecs, w_out_l, w_gu_l, w_down_l, tm=POST_TILE)

    return (xp.reshape(n_ctx, ctx_len, D_MODEL), xs,
            jnp.stack(new_k, axis=1), jnp.stack(new_v, axis=1), jnp.stack(new_h, axis=1))
```

```python
import functools
import math

import jax
import jax.numpy as jnp
import numpy as np
from jax import lax
from jax.experimental import pallas as pl
from jax.experimental.pallas import tpu as pltpu

D_MODEL = 1024
DEPTH = 2
GRID_W = 64
HEAD_DIM = 64
N_HEADS = 4
HEAD_W = 2 * HEAD_DIM
ATTN_W = N_HEADS * HEAD_W
LRU_W = D_MODEL // 4
LRU_BLOCKS = 4
CONV_W = 4
FOURIER_W = D_MODEL // 4
FOURIER_GROUPS = 4
FOURIER_GW = FOURIER_W // FOURIER_GROUPS
MIX_W = ATTN_W + LRU_W + FOURIER_W
REST_W = 2 * LRU_W
IN_W = 3 * ATTN_W + REST_W + FOURIER_W
FFN_HIDDEN = -(-8 * D_MODEL // (3 * 256)) * 256
ROPE_BASE = 10000.0
RG_C = 8.0
EPS = 1e-6
N_MOD = 6
COND_ROWS = 8

F32 = jnp.float32
BF16 = jnp.bfloat16
V7X_VMEM_LIMIT_BYTES = 56 * 1024 * 1024
CONV_HALO = 8


def _cparams(*semantics, flags=None):
    return pltpu.CompilerParams(dimension_semantics=semantics,
                                vmem_limit_bytes=V7X_VMEM_LIMIT_BYTES, flags=flags)


def _resident(shape, index_map):
    return pl.BlockSpec(shape, index_map, pipeline_mode=pl.Buffered(1))


def _lambda_init(l):
    return 0.8 - 0.6 * math.exp(-0.3 * l)


def _rms(x, axis):
    return x * lax.rsqrt(jnp.mean(x * x, axis=axis, keepdims=True) + EPS)


def _mod_kernel(cond_ref, w_ref, b_ref, o_ref):
    s = cond_ref[...]
    s = s * jax.nn.sigmoid(s)
    o_ref[0] = jnp.dot(s, w_ref[0], precision=lax.Precision.HIGHEST,
                       preferred_element_type=F32) + b_ref[0]


def _modulation(cond, w_mod, b_mod):
    tn = 1536
    n_out = N_MOD * D_MODEL
    return pl.pallas_call(
        _mod_kernel,
        grid=(DEPTH, n_out // tn),
        in_specs=[pl.BlockSpec((COND_ROWS, D_MODEL), lambda l, j: (0, 0)),
                  pl.BlockSpec((1, D_MODEL, tn), lambda l, j: (l, 0, j)),
                  pl.BlockSpec((1, 1, tn), lambda l, j: (l, 0, j))],
        out_specs=pl.BlockSpec((1, COND_ROWS, tn), lambda l, j: (l, 0, j)),
        out_shape=jax.ShapeDtypeStruct((DEPTH, COND_ROWS, n_out), F32),
        compiler_params=_cparams("arbitrary", "arbitrary"),
        name="modulation",
    )(cond, w_mod, b_mod.reshape(DEPTH, 1, n_out))


INPROJ_TILE = 512
INPROJ_ROW_GROUP = 256


def _inproj_kernel(*refs, rope, kv_seq_len, n_alias):
    x_ref, mod_ref, g_ref, w_ref = refs[:4]
    refs = refs[4:]
    if rope:
        cos_ref, sin_lo_ref, sin_hi_ref = refs[:3]
        refs = refs[3:]
    refs = refs[n_alias:]
    qt_ref, k_ref, vt_ref, rest_ref, xf_ref = refs[:5]
    mod = mod_ref[0]
    tm = x_ref.shape[1]
    groups = [slice(r, r + INPROJ_ROW_GROUP) for r in range(0, tm, INPROJ_ROW_GROUP)]
    hs = [((_rms(x_ref[0, rows], -1) * g_ref[...]) * (1.0 + mod[1:2]) + mod[0:1]).astype(BF16)
          for rows in groups]
    projs = [jnp.dot(h, w_ref[...], preferred_element_type=F32) for h in hs]

    row = lax.broadcasted_iota(jnp.int32, (HEAD_W, INPROJ_ROW_GROUP), 0)
    scale = HEAD_DIM ** -0.5 * math.log2(math.e)
    for gi, (rows, proj) in enumerate(zip(groups, projs)):
        q = proj[:, :ATTN_W]
        k = proj[:, ATTN_W:2 * ATTN_W]
        v = proj[:, 2 * ATTN_W:3 * ATTN_W]
        if kv_seq_len is not None:
            assert kv_seq_len == INPROJ_ROW_GROUP
            kf_ref, vf_ref = refs[5:7]
            kf_ref[gi] = k
            vf_ref[gi] = v

        def rotary(t):
            return (t * cos_ref[rows] + pltpu.roll(t, HEAD_W - 16, 1) * sin_lo_ref[rows]
                    + pltpu.roll(t, 16, 1) * sin_hi_ref[rows])

        for hd in range(N_HEADS):
            qh = q[:, hd * HEAD_W:(hd + 1) * HEAD_W]
            kh = k[:, hd * HEAD_W:(hd + 1) * HEAD_W]
            if rope:
                qh = rotary(qh)
                kh = rotary(kh)
            qht = (qh * scale).T
            qt_ref[0, 2 * hd, :, rows] = jnp.where(row < HEAD_DIM, qht, 0.0).astype(BF16)
            qt_ref[0, 2 * hd + 1, :, rows] = jnp.where(row >= HEAD_DIM, qht, 0.0).astype(BF16)
            k_ref[0, rows, hd * HEAD_W:(hd + 1) * HEAD_W] = kh.astype(BF16)
        vt_ref[0, :, rows] = v.T.astype(BF16)
        rest_ref[0, rows] = proj[:, 3 * ATTN_W:3 * ATTN_W + REST_W]
        xf_ref[0, rows] = proj[:, 3 * ATTN_W + REST_W:].astype(BF16)


def _inproj(x, mod, goff, g_pre, w_in, rope_tabs, kv_out=None):
    G, T, _ = x.shape
    tm = INPROJ_TILE
    rope = rope_tabs is not None
    in_specs = [pl.BlockSpec((1, tm, D_MODEL), lambda g, i: (g, i, 0)),
                pl.BlockSpec((1, N_MOD, D_MODEL), lambda g, i: (g + goff, 0, 0)),
                pl.BlockSpec((1, D_MODEL), lambda g, i: (0, 0)),
                _resident((D_MODEL, IN_W), lambda g, i: (0, 0))]
    args = [x, mod, g_pre, w_in]
    if rope:
        in_specs += [pl.BlockSpec((tm, HEAD_W), lambda g, i: (i, 0))] * 3
        args += list(rope_tabs)
    out_specs = [pl.BlockSpec((1, 2 * N_HEADS, HEAD_W, tm), lambda g, i: (g, 0, 0, i)),
                 pl.BlockSpec((1, tm, ATTN_W), lambda g, i: (g, i, 0)),
                 pl.BlockSpec((1, ATTN_W, tm), lambda g, i: (g, 0, i)),
                 pl.BlockSpec((1, tm, REST_W), lambda g, i: (g, i, 0)),
                 pl.BlockSpec((1, tm, FOURIER_W), lambda g, i: (g, i, 0))]
    out_shape = [jax.ShapeDtypeStruct((G, 2 * N_HEADS, HEAD_W, T), BF16),
                 jax.ShapeDtypeStruct((G, T, ATTN_W), BF16),
                 jax.ShapeDtypeStruct((G, ATTN_W, T), BF16),
                 jax.ShapeDtypeStruct((G, T, REST_W), F32),
                 jax.ShapeDtypeStruct((G, T, FOURIER_W), BF16)]
    aliases = {}
    kv_seq_len = None
    n_alias = 0
    if kv_out is not None:
        assert G == 1
        layer, kv_seq_len, n_layers, bufs = kv_out
        per_tile = tm // kv_seq_len
        out_specs += [pl.BlockSpec((per_tile, None, kv_seq_len, ATTN_W),
                                   lambda g, i: (i, layer, 0, 0))] * 2
        out_shape += [jax.ShapeDtypeStruct((T // kv_seq_len, n_layers, kv_seq_len, ATTN_W), F32)] * 2
        if bufs is not None:
            n_alias = len(bufs)
            aliases = {len(args) + j: 5 + j for j in range(n_alias)}
            in_specs += [pl.BlockSpec(memory_space=pl.ANY)] * n_alias
            args += list(bufs)
    return pl.pallas_call(
        functools.partial(_inproj_kernel, rope=rope, kv_seq_len=kv_seq_len, n_alias=n_alias),
        grid=(G, T // tm),
        in_specs=in_specs, out_specs=out_specs, out_shape=out_shape,
        input_output_aliases=aliases,
        compiler_params=_cparams("arbitrary", "arbitrary"),
        name="inproj_rope" if rope else "inproj_ctx",
    )(*args)


QUERY_BLOCK = 256
KEY_CHUNK = 256
SUBLANES = 8
SUM_ACCUMULATORS = 2


ROW_ACCUMULATORS = 2


def _accumulate_rows(accs, x, op):
    for idx, r in enumerate(range(0, x.shape[0], SUBLANES)):
        j = idx % len(accs)
        part = x[r:r + SUBLANES]
        accs[j] = part if accs[j] is None else op(accs[j], part)


def _finish_rows(accs, op, reduce):
    total = accs[0]
    for a in accs[1:]:
        total = op(total, a)
    return reduce(total, axis=0, keepdims=True)


def _attn_kernel(*refs, has_cache, li, n_heads, n_sub, n_keys):
    qt_ref, k_ref, vt_ref = refs[:3]
    refs = refs[3:]
    if has_cache:
        ck_ref, cv_ref = refs[:2]
        refs = refs[2:]
    wl_ref, g_ref, o_ref = refs[:3]
    s_bufs = refs[3:5]
    acc_bufs = refs[5:7]
    n_self = k_ref.shape[1]
    if has_cache:
        kall, vtall = refs[7:9]

        @pl.when(pl.program_id(2) == 0)
        def _():
            kall[0:n_self] = k_ref[0]
            kall[n_self:n_keys] = ck_ref[0].astype(BF16)
            vtall[:, 0:n_self] = vt_ref[0]
            vtall[:, n_self:n_keys] = cv_ref[0].T.astype(BF16)

    def k_chunk(hd, c):
        rows = slice(c * KEY_CHUNK, (c + 1) * KEY_CHUNK)
        return kall[rows, :] if has_cache else k_ref[0, rows, hd * HEAD_W:(hd + 1) * HEAD_W]

    def vt_chunk(hd, c):
        cols = slice(c * KEY_CHUNK, (c + 1) * KEY_CHUNK)
        return vtall[:, cols] if has_cache else vt_ref[0, hd * HEAD_W:(hd + 1) * HEAD_W, cols]

    units = [(hd, sub, m) for hd in range(n_heads) for sub in range(n_sub) for m in range(2)]
    n_chunks = n_keys // KEY_CHUNK
    results = {}
    col_max = None
    for t in range(len(units) + 1):
        if t < len(units):
            hd, sub, m = units[t]
            qt = qt_ref[0, 2 * hd + m, :, sub * QUERY_BLOCK:(sub + 1) * QUERY_BLOCK]
            new_max = [None] * ROW_ACCUMULATORS
        if t > 0:
            hd_p = units[t - 1][0]
            mx = _finish_rows(col_max, jnp.maximum, jnp.max)
            den = [None] * SUM_ACCUMULATORS
            acc_ref = acc_bufs[(t - 1) % 2]
        for c in range(n_chunks):
            rows = slice(c * KEY_CHUNK, (c + 1) * KEY_CHUNK)
            if t < len(units):
                s = jnp.dot(k_chunk(hd, c), qt, preferred_element_type=F32)
                s_bufs[t % 2][rows, :] = s
                _accumulate_rows(new_max, s, jnp.maximum)
            if t > 0:
                e = jnp.exp2(s_bufs[(t - 1) % 2][rows, :] - mx)
                _accumulate_rows(den, e, jnp.add)
                pv = jnp.dot(vt_chunk(hd_p, c), e.astype(BF16), preferred_element_type=F32)
                if c == 0:
                    acc_ref[...] = pv
                else:
                    acc_ref[...] += pv
        if t > 0:
            results[units[t - 1]] = acc_ref[...] / _finish_rows(den, jnp.add, jnp.sum)
        col_max = new_max

    wl = wl_ref[...]
    lam = (jnp.exp(jnp.sum(wl[0:1] * wl[1:2], axis=-1, keepdims=True))
           - jnp.exp(jnp.sum(wl[2:3] * wl[3:4], axis=-1, keepdims=True)) + li)
    for hd in range(n_heads):
        for sub in range(n_sub):
            ot = results[(hd, sub, 0)] - lam * results[(hd, sub, 1)]
            y = (_rms(ot, 0) * g_ref[...]) * (1.0 - li)
            o_ref[0, sub * QUERY_BLOCK:(sub + 1) * QUERY_BLOCK,
                  hd * HEAD_W:(hd + 1) * HEAD_W] = y.T.astype(BF16)


def _attention(qt, k, vt, cache, w_lambda_l, g_subln_l, li, n_seq, seq_len, n_heads, n_sub):
    G, _, _, T = qt.shape
    per_g = T // seq_len
    tq = n_sub * QUERY_BLOCK
    nq = seq_len // tq
    has_cache = cache is not None
    n_keys = seq_len

    def gi(b):
        return b // per_g

    def si(b):
        return b % per_g

    hw = n_heads * HEAD_W
    in_specs = [pl.BlockSpec((1, 2 * n_heads, HEAD_W, tq), lambda b, h, i: (gi(b), h, 0, si(b) * nq + i)),
                pl.BlockSpec((1, seq_len, hw), lambda b, h, i: (gi(b), si(b), h)),
                pl.BlockSpec((1, hw, seq_len), lambda b, h, i: (gi(b), h, si(b)))]
    args = [qt, k, vt]
    if has_cache:
        assert n_heads == 1
        ck, cv, layer = cache
        past = ck.shape[2]
        n_keys = seq_len + past
        in_specs += [pl.BlockSpec((1, None, past, HEAD_W), lambda b, h, i: (b, layer, 0, h))] * 2
        args += [ck, cv]
    scratch = [pltpu.VMEM((n_keys, QUERY_BLOCK), F32)] * 2
    scratch += [pltpu.VMEM((HEAD_W, QUERY_BLOCK), F32)] * 2
    if has_cache:
        scratch += [pltpu.VMEM((n_keys, HEAD_W), BF16), pltpu.VMEM((HEAD_W, n_keys), BF16)]
    in_specs += [pl.BlockSpec((4, HEAD_DIM), lambda b, h, i: (0, 0)),
                 pl.BlockSpec((HEAD_W, 1), lambda b, h, i: (0, 0))]
    args += [w_lambda_l, g_subln_l.reshape(HEAD_W, 1)]
    return pl.pallas_call(
        functools.partial(_attn_kernel, has_cache=has_cache, li=li, n_heads=n_heads,
                          n_sub=n_sub, n_keys=n_keys),
        grid=(n_seq, N_HEADS // n_heads, nq),
        in_specs=in_specs,
        out_specs=pl.BlockSpec((1, tq, hw), lambda b, h, i: (gi(b), si(b) * nq + i, h)),
        out_shape=jax.ShapeDtypeStruct((G, T, ATTN_W), BF16),
        scratch_shapes=scratch,
        compiler_params=_cparams("arbitrary", "arbitrary", "arbitrary"),
        name="attn_latent" if has_cache else "attn_ctx",
    )(*args)


def _scan_chunk(a, b, carry, reverse):
    n, w = a.shape
    groups = n // SUBLANES
    a = a.reshape(groups, SUBLANES, w)
    b = b.reshape(groups, SUBLANES, w)
    sub = lax.broadcasted_iota(jnp.int32, (groups, SUBLANES, w), 1)
    d = 1
    while d < SUBLANES:
        if reverse:
            shift, valid = SUBLANES - d, sub < SUBLANES - d
        else:
            shift, valid = d, sub >= d
        a_prev = jnp.where(valid, pltpu.roll(a, shift, 1), 1.0)
        b_prev = jnp.where(valid, pltpu.roll(b, shift, 1), 0.0)
        b = a * b_prev + b
        a = a * a_prev
        d *= 2
    edge = 0 if reverse else SUBLANES - 1
    hs = [None] * groups
    for g in (range(groups - 1, -1, -1) if reverse else range(groups)):
        hs[g] = a[g] * carry + b[g]
        carry = hs[g][edge:edge + 1]
    return jnp.concatenate(hs, axis=0), carry


def _lru_kernel(xr_ref, gr_ref, cw_ref, cb_ref, w_ref, bias_ref, lam_ref, h0_ref,
                out_ref, hfin_ref, xpad, hf_scr, ab_scr, bb_scr, *, seq_len, tc):
    nch = seq_len // tc
    zeros_halo = jnp.zeros((CONV_HALO, LRU_W), F32)
    xpad[0:CONV_HALO] = zeros_halo
    xpad[seq_len + CONV_HALO:seq_len + 2 * CONV_HALO] = zeros_halo
    xpad[CONV_HALO:seq_len + CONV_HALO] = xr_ref[0]
    lam_p = lam_ref[...]
    neg = -lam_p
    softplus = jnp.maximum(neg, 0.0) + jnp.log1p(jnp.exp(-jnp.abs(neg)))
    coef = -RG_C * softplus
    cw = cw_ref[...]
    h0 = h0_ref[0]

    def decay_and_input(xc, z, direction):
        off = direction * 2 * LRU_W
        r = jax.nn.sigmoid(z[:, off:off + LRU_W])
        i = jax.nn.sigmoid(z[:, off + LRU_W:off + 2 * LRU_W])
        log_a = coef[direction:direction + 1] * r
        a = jnp.exp(log_a)
        b = jnp.sqrt(-jnp.tanh(log_a) * (1.0 + a * a)) * (i * xc)
        return a, b

    def forward_chunk(c, carry):
        start = pl.multiple_of(c * tc, tc)
        win = xpad[pl.ds(start, tc + 2 * CONV_HALO), :]
        lo = CONV_HALO - CONV_W // 2
        xc = win[lo:lo + tc] * cw[0:1]
        for j in range(1, CONV_W):
            xc = xc + win[lo + j:lo + j + tc] * cw[j:j + 1]
        xc = xc + cb_ref[...]
        z = jnp.dot(xc.astype(BF16), w_ref[...], preferred_element_type=F32) + bias_ref[...]
        a_f, b_f = decay_and_input(xc, z, 0)
        hf, carry = _scan_chunk(a_f, b_f, carry, reverse=False)
        hf_scr[pl.ds(start, tc), :] = hf
        a_b, b_b = decay_and_input(xc, z, 1)
        ab_scr[pl.ds(start, tc), :] = a_b
        bb_scr[pl.ds(start, tc), :] = b_b
        return carry

    def backward_chunk(j, carry):
        c = nch - 1 - j
        start = pl.multiple_of(c * tc, tc)
        hb, carry = _scan_chunk(ab_scr[pl.ds(start, tc), :], bb_scr[pl.ds(start, tc), :],
                                carry, reverse=True)
        gate = jax.nn.gelu(gr_ref[0, pl.ds(start, tc), :], approximate=True)
        out_ref[0, pl.ds(start, tc), :] = ((hf_scr[pl.ds(start, tc), :] + hb) * gate).astype(BF16)
        return carry

    if nch == 1:
        hf_fin = forward_chunk(0, h0[0:1])
        hb_fin = backward_chunk(0, h0[1:2])
    else:
        hf_fin = lax.fori_loop(0, nch, forward_chunk, h0[0:1])
        hb_fin = lax.fori_loop(0, nch, backward_chunk, h0[1:2])
    hfin_ref[0, 0:1] = hf_fin
    hfin_ref[0, 1:2] = hb_fin


def _rg_lru(rest, conv_w, conv_b, w_gates, b_gates, lam_p, h0):
    B, L, _ = rest.shape
    tc = min(L, 256)
    return pl.pallas_call(
        functools.partial(_lru_kernel, seq_len=L, tc=tc),
        grid=(B,),
        in_specs=[pl.BlockSpec((1, L, LRU_W), lambda b: (b, 0, 0)),
                  pl.BlockSpec((1, L, LRU_W), lambda b: (b, 0, 1)),
                  pl.BlockSpec((CONV_W, LRU_W), lambda b: (0, 0)),
                  pl.BlockSpec((1, LRU_W), lambda b: (0, 0)),
                  pl.BlockSpec((LRU_W, 4 * LRU_W), lambda b: (0, 0)),
                  pl.BlockSpec((1, 4 * LRU_W), lambda b: (0, 0)),
                  pl.BlockSpec((2, LRU_W), lambda b: (0, 0)),
                  pl.BlockSpec((1, 2, LRU_W), lambda b: (b, 0, 0))],
        out_specs=[pl.BlockSpec((1, L, LRU_W), lambda b: (b, 0, 0)),
                   pl.BlockSpec((1, 2, LRU_W), lambda b: (b, 0, 0))],
        out_shape=[jax.ShapeDtypeStruct((B, L, LRU_W), BF16),
                   jax.ShapeDtypeStruct((B, 2, LRU_W), F32)],
        scratch_shapes=[pltpu.VMEM((L + 2 * CONV_HALO, LRU_W), F32),
                        pltpu.VMEM((L, LRU_W), F32),
                        pltpu.VMEM((L, LRU_W), F32),
                        pltpu.VMEM((L, LRU_W), F32)],
        compiler_params=_cparams("arbitrary"),
        name="rg_lru",
    )(rest, rest, conv_w, conv_b.reshape(1, LRU_W), w_gates, b_gates, lam_p, h0)


def _chan_dft_kernel(x_ref, w_ref, yc_ref, ys_ref):
    y = jnp.dot(x_ref[0], w_ref[...], preferred_element_type=F32)
    yc_ref[...] = y[:, :FOURIER_W].astype(BF16)
    ys_ref[...] = y[:, FOURIER_W:].astype(BF16)


def _seq_dft_kernel(fc_ref, fs_ref, yc_ref, ys_ref, o_ref, *, scale):
    acc = jnp.dot(fc_ref[...], yc_ref[...], preferred_element_type=F32)
    acc = acc - jnp.dot(fs_ref[...], ys_ref[...], preferred_element_type=F32)
    o_ref[...] = (acc * scale).astype(BF16)


def _fourier_direct(xf, chan_dft, seq_cos, seq_sin):
    B, L, _ = xf.shape
    y_shape = jax.ShapeDtypeStruct((L, B * FOURIER_W), BF16)
    yc, ys = pl.pallas_call(
        _chan_dft_kernel,
        grid=(B,),
        in_specs=[pl.BlockSpec((1, L, FOURIER_W), lambda b: (b, 0, 0)),
                  pl.BlockSpec((FOURIER_W, 2 * FOURIER_W), lambda b: (0, 0))],
        out_specs=[pl.BlockSpec((L, FOURIER_W), lambda b: (0, b))] * 2,
        out_shape=[y_shape, y_shape],
        compiler_params=_cparams("arbitrary"),
        name="fourier_channels",
    )(xf, chan_dft)
    return pl.pallas_call(
        functools.partial(_seq_dft_kernel, scale=(L * FOURIER_GW) ** -0.5),
        grid=(B,),
        in_specs=[pl.BlockSpec((L, L), lambda b: (0, 0)),
                  pl.BlockSpec((L, L), lambda b: (0, 0)),
                  pl.BlockSpec((L, FOURIER_W), lambda b: (0, b)),
                  pl.BlockSpec((L, FOURIER_W), lambda b: (0, b))],
        out_specs=pl.BlockSpec((L, FOURIER_W), lambda b: (b, 0)),
        out_shape=jax.ShapeDtypeStruct((B * L, FOURIER_W), BF16),
        compiler_params=_cparams("arbitrary"),
        name="fourier_sequence",
    )(seq_cos, seq_sin, yc, ys)


FFT_RADIX = 64
FFT_COLS = 2048
FFT_K1_PER_STEP = 8


def _fft_stage1_kernel(x_ref, f_ref, tc_ref, ts_ref, o_ref):
    a = jnp.dot(f_ref[...], x_ref[0], preferred_element_type=F32)
    a_re, a_im = a[:FFT_RADIX], a[FFT_RADIX:]
    tc, ts = tc_ref[...], ts_ref[...]
    o_ref[0, 0] = (a_re * tc + a_im * ts).astype(BF16)
    o_ref[0, 1] = (a_im * tc - a_re * ts).astype(BF16)


def _fft_stage2_kernel(b_ref, f_ref, w_ref, o_ref, *, scale):
    zs = []
    for j in range(FFT_K1_PER_STEP):
        bj = jnp.concatenate([b_ref[0, 0, j], b_ref[0, 1, j]], axis=0)
        z = jnp.dot(f_ref[...], bj, preferred_element_type=F32)
        zs.append(jnp.concatenate([z[:FFT_RADIX], z[FFT_RADIX:]], axis=1))
    z_all = jnp.concatenate(zs, axis=0).astype(BF16)
    out = jnp.dot(z_all, w_ref[...], preferred_element_type=F32) * scale
    o_ref[0] = out.reshape(FFT_K1_PER_STEP, FFT_RADIX, FOURIER_W).astype(BF16)


def _fourier_fft(xf, stage1_dft, twiddle_cos, twiddle_sin, stage2_dft, chan_dft_stacked):
    B, L, _ = xf.shape
    R = FFT_RADIX
    assert L == R * R
    cols = R * FOURIER_W
    x2 = xf.reshape(B, R, cols)
    mid = pl.pallas_call(
        _fft_stage1_kernel,
        grid=(cols // FFT_COLS, B),
        in_specs=[pl.BlockSpec((1, R, FFT_COLS), lambda j, b: (b, 0, j)),
                  pl.BlockSpec((2 * R, R), lambda j, b: (0, 0)),
                  pl.BlockSpec((R, FFT_COLS), lambda j, b: (0, j)),
                  pl.BlockSpec((R, FFT_COLS), lambda j, b: (0, j))],
        out_specs=pl.BlockSpec((1, 2, R, FFT_COLS), lambda j, b: (b, 0, 0, j)),
        out_shape=jax.ShapeDtypeStruct((B, 2, R, cols), BF16),
        compiler_params=_cparams("arbitrary", "arbitrary"),
        name="fft_stage1",
    )(x2, stage1_dft, twiddle_cos, twiddle_sin)
    mid = mid.reshape(B, 2, R, R, FOURIER_W)
    out = pl.pallas_call(
        functools.partial(_fft_stage2_kernel, scale=(L * FOURIER_GW) ** -0.5),
        grid=(B, R // FFT_K1_PER_STEP),
        in_specs=[pl.BlockSpec((1, 2, FFT_K1_PER_STEP, R, FOURIER_W), lambda b, i: (b, 0, i, 0, 0)),
                  pl.BlockSpec((2 * R, 2 * R), lambda b, i: (0, 0)),
                  pl.BlockSpec((2 * FOURIER_W, FOURIER_W), lambda b, i: (0, 0))],
        out_specs=pl.BlockSpec((1, FFT_K1_PER_STEP, R, FOURIER_W), lambda b, i: (b, i, 0, 0)),
        out_shape=jax.ShapeDtypeStruct((B, R, R, FOURIER_W), BF16),
        compiler_params=_cparams("arbitrary", "arbitrary"),
        name="fft_stage2",
    )(mid, stage2_dft, chan_dft_stacked)
    return out.transpose(0, 2, 1, 3).reshape(B, L, FOURIER_W)


POST_TILE = 512
POST_ROW_GROUP = 256
def _post_kernel(x_ref, attn_ref, lru_ref, four_ref, mod_ref, gmix_ref, gpre_ref, gffn_ref,
                 wo_ref, wgu_ref, wd_ref, o_ref):
    mod = mod_ref[0]
    tm = x_ref.shape[1]
    groups = [slice(r, r + POST_ROW_GROUP) for r in range(0, tm, POST_ROW_GROUP)]

    def mix_out(rows):
        o = jnp.dot(attn_ref[0, rows], wo_ref[0:ATTN_W], preferred_element_type=F32)
        o = o + jnp.dot(lru_ref[0, rows], wo_ref[ATTN_W:ATTN_W + LRU_W], preferred_element_type=F32)
        return o + jnp.dot(four_ref[0, rows], wo_ref[ATTN_W + LRU_W:MIX_W], preferred_element_type=F32)

    def residual_and_prenorm(rows, o):
        x1 = x_ref[0, rows] + mod[2:3] * (_rms(o, -1) * gmix_ref[...])
        h = (_rms(x1, -1) * gpre_ref[...]) * (1.0 + mod[4:5]) + mod[3:4]
        return x1, h.astype(BF16)

    def gate_up(hb):
        gt = jnp.dot(hb, wgu_ref[:, :FFN_HIDDEN], preferred_element_type=F32)
        up = jnp.dot(hb, wgu_ref[:, FFN_HIDDEN:], preferred_element_type=F32)
        return ((gt * jax.nn.sigmoid(gt)) * up).astype(BF16)

    mixed = [mix_out(rows) for rows in groups]
    normed = [residual_and_prenorm(rows, o) for rows, o in zip(groups, mixed)]
    acts = [gate_up(hb) for _, hb in normed]
    downs = [jnp.dot(act, wd_ref[...], preferred_element_type=F32) for act in acts]
    for rows, (x1, _), f in zip(groups, normed, downs):
        o_ref[0, rows] = x1 + mod[5:6] * (_rms(f, -1) * gffn_ref[...])


def _post(x, attn, lru, four, mod, goff, g_post_mix, g_pre_ffn, g_post_ffn,
          w_out, w_gate_up, w_down, tm):
    G, T, _ = x.shape
    vec = pl.BlockSpec((1, D_MODEL), lambda g, i: (0, 0))
    return pl.pallas_call(
        _post_kernel,
        grid=(G, T // tm),
        in_specs=[pl.BlockSpec((1, tm, D_MODEL), lambda g, i: (g, i, 0)),
                  pl.BlockSpec((1, tm, ATTN_W), lambda g, i: (g, i, 0)),
                  pl.BlockSpec((1, tm, LRU_W), lambda g, i: (g, i, 0)),
                  pl.BlockSpec((1, tm, FOURIER_W), lambda g, i: (g, i, 0)),
                  pl.BlockSpec((1, N_MOD, D_MODEL), lambda g, i: (g + goff, 0, 0)),
                  vec, vec, vec,
                  _resident((MIX_W, D_MODEL), lambda g, i: (0, 0)),
                  _resident((D_MODEL, 2 * FFN_HIDDEN), lambda g, i: (0, 0)),
                  _resident((FFN_HIDDEN, D_MODEL), lambda g, i: (0, 0))],
        out_specs=pl.BlockSpec((1, tm, D_MODEL), lambda g, i: (g, i, 0)),
        out_shape=jax.ShapeDtypeStruct((G, T, D_MODEL), F32),
        compiler_params=_cparams("arbitrary", "arbitrary"),
        name="post",
    )(x, attn, lru, four, mod, g_post_mix, g_pre_ffn, g_post_ffn, w_out, w_gate_up, w_down)


def _rope_tables(n_tokens):
    rows = n_tokens // GRID_W
    row = jnp.repeat(jnp.arange(rows, dtype=F32), GRID_W)
    col = jnp.tile(jnp.arange(GRID_W, dtype=F32), rows)
    n = HEAD_DIM // 4
    inv = ROPE_BASE ** (-jnp.arange(n, dtype=F32) / n)
    ang_r, ang_c = row[:, None] * inv, col[:, None] * inv
    zero = jnp.zeros_like(ang_r)
    cos64 = jnp.concatenate([jnp.cos(ang_r)] * 2 + [jnp.cos(ang_c)] * 2, axis=-1)
    sin_lo = jnp.concatenate([-jnp.sin(ang_r), zero, -jnp.sin(ang_c), zero], axis=-1)
    sin_hi = jnp.concatenate([zero, jnp.sin(ang_r), zero, jnp.sin(ang_c)], axis=-1)
    return tuple(jnp.tile(t, (1, 2)) for t in (cos64, sin_lo, sin_hi))


def _seq_dft_tables(n):
    r = 1 << (n.bit_length() // 2)
    l = jnp.arange(n, dtype=jnp.int32)

    def cos_sin(rows, period):
        k = jnp.arange(rows, dtype=jnp.int32)
        ang = ((k[:, None] * l[None, :]) % period).astype(F32) * (2.0 * math.pi / period)
        return jnp.cos(ang), jnp.sin(ang)

    ca, sa = cos_sin(n // r, n // r)
    cb, sb = cos_sin(r, n)
    cos = ca[:, None, :] * cb[None] - sa[:, None, :] * sb[None]
    sin = sa[:, None, :] * cb[None] + ca[:, None, :] * sb[None]
    return cos.reshape(n, n).astype(BF16), sin.reshape(n, n).astype(BF16)


def _chan_dft_table():
    k = np.arange(FOURIER_GW)
    ang = 2.0 * np.pi * ((k[:, None] * k[None, :]) % FOURIER_GW) / FOURIER_GW
    eye = np.eye(FOURIER_GROUPS)
    table = np.concatenate([np.kron(eye, np.cos(ang)), np.kron(eye, np.sin(ang))], axis=1)
    return jnp.asarray(table, dtype=F32).astype(BF16)


def _fft_tables(n):
    r = FFT_RADIX
    k = np.arange(r)
    ang = 2.0 * np.pi * ((k[:, None] * k[None, :]) % r) / r
    cm, sm = np.cos(ang), np.sin(ang)
    stage1 = np.concatenate([cm, -sm], axis=0)
    stage2 = np.block([[cm, sm], [-sm, cm]])
    kj = jnp.arange(r, dtype=jnp.int32)
    tw = (kj[:, None] * kj[None, :]).astype(F32) * (2.0 * math.pi / n)
    expand = lambda t: jnp.repeat(t, FOURIER_W, axis=1)
    return (jnp.asarray(stage1, dtype=F32).astype(BF16), expand(jnp.cos(tw)), expand(jnp.sin(tw)),
            jnp.asarray(stage2, dtype=F32).astype(BF16))


def _block_diag(w):
    n, bw, _ = w.shape
    eye = jnp.eye(n, dtype=w.dtype)
    return (eye[:, None, :, None] * w[:, :, None, :]).reshape(n * bw, n * bw)


def kernel(x_prompt, x_sample, cache_k, cache_v, state_lru, c, c_ctx, w_mod, b_mod, g_pre_mix, g_post_mix, g_pre_ffn, g_post_ffn, w_in, w_out, w_lambda, g_subln, conv_w, conv_b, lru_wa, lru_ba, lru_wx, lru_bx, lru_lambda, w_gate_up, w_down):
    n_ctx, ctx_len, _ = x_prompt.shape
    n_lat, lat_len, _ = x_sample.shape
    past = cache_k.shape[2]

    cond = jnp.concatenate(
        [c_ctx[None, :], c, jnp.zeros((COND_ROWS - 1 - n_lat, D_MODEL), F32)], axis=0)
    mod_all = _modulation(cond, w_mod, b_mod).reshape(DEPTH, COND_ROWS, N_MOD, D_MODEL)

    rope_tabs = _rope_tables(lat_len)
    chan_dft = _chan_dft_table()
    chan_dft_stacked = jnp.concatenate([chan_dft[:, :FOURIER_W], chan_dft[:, FOURIER_W:]], axis=0)
    dft_ctx = _seq_dft_tables(ctx_len)
    fft_lat = _fft_tables(lat_len)
    ck = cache_k.reshape(n_lat, DEPTH, past, ATTN_W)
    cv = cache_v.reshape(n_lat, DEPTH, past, ATTN_W)
    h0_ctx = jnp.zeros((n_ctx, 2, LRU_W), F32)

    xp = x_prompt.reshape(1, n_ctx * ctx_len, D_MODEL)
    xs = x_sample
    kv_bufs, new_h = None, []
    for l in range(DEPTH):
        li = _lambda_init(l)
        mod = mod_all[l]
        w_in_l = w_in[l].astype(BF16)
        w_out_l = w_out[l].astype(BF16)
        w_gu_l = w_gate_up[l].astype(BF16)
        w_down_l = w_down[l].astype(BF16)
        w_gates = jnp.concatenate(
            [_block_diag(lru_wa[l, 0]), _block_diag(lru_wx[l, 0]),
             _block_diag(lru_wa[l, 1]), _block_diag(lru_wx[l, 1])], axis=1).astype(BF16)
        b_gates = jnp.concatenate(
            [lru_ba[l, 0], lru_bx[l, 0], lru_ba[l, 1], lru_bx[l, 1]])[None, :]
        vecs = [g[l][None, :] for g in (g_post_mix, g_pre_ffn, g_post_ffn)]

        qt, k, vt, rest, xf, *kv_bufs = _inproj(
            xp, mod, 0, g_pre_mix[l][None, :], w_in_l, None,
            kv_out=(l, ctx_len, DEPTH, kv_bufs))
        attn = _attention(qt, k, vt, None, w_lambda[l], g_subln[l], li,
                          n_seq=n_ctx, seq_len=ctx_len, n_heads=N_HEADS,
                          n_sub=ctx_len // QUERY_BLOCK)
        rest_seq = rest.reshape(n_ctx, ctx_len, REST_W)
        lru, h_fin = _rg_lru(rest_seq, conv_w[l], conv_b[l], w_gates, b_gates,
                             lru_lambda[l], h0_ctx)
        new_h.append(h_fin)
        four = _fourier_direct(xf.reshape(n_ctx, ctx_len, FOURIER_W), chan_dft, *dft_ctx)
        xp = _post(xp, attn, lru.reshape(1, n_ctx * ctx_len, LRU_W),
                   four.reshape(1, n_ctx * ctx_len, FOURIER_W), mod, 0,
                   *vecs, w_out_l, w_gu_l, w_down_l, tm=POST_TILE)

        qt, k, vt, rest, xf = _inproj(
            xs, mod, 1, g_pre_mix[l][None, :], w_in_l, rope_tabs)
        attn = _attention(qt, k, vt, (ck, cv, l), w_lambda[l], g_subln[l], li,
                          n_seq=n_lat, seq_len=lat_len, n_heads=1, n_sub=4)
        lru, _ = _rg_lru(rest, conv_w[l], conv_b[l], w_gates, b_gates,
                         lru_lambda[l], state_lru[:, l])
        four = _fourier_fft(xf, fft_lat[0], fft_lat[1], fft_lat[2], fft_lat[3], chan_dft_stacked)
        xs = _post(xs, attn, lru, four, mod, 1,
                   *vecs, w_out_l, w_gu_l, w_down_l, tm=POST_TILE)

    return (xp.reshape(n_ctx, ctx_len, D_MODEL), xs,
            kv_bufs[0].reshape(n_ctx, DEPTH, ctx_len, N_HEADS, 2, HEAD_DIM),
            kv_bufs[1].reshape(n_ctx, DEPTH, ctx_len, N_HEADS, HEAD_W), jnp.stack(new_h, axis=1))
```

```python
import functools
import math

import jax
import jax.numpy as jnp
import numpy as np
from jax import lax
from jax.experimental import pallas as pl
from jax.experimental.pallas import tpu as pltpu

D_MODEL = 1024
DEPTH = 2
GRID_W = 64
HEAD_DIM = 64
N_HEADS = 4
HEAD_W = 2 * HEAD_DIM
ATTN_W = N_HEADS * HEAD_W
LRU_W = D_MODEL // 4
LRU_BLOCKS = 4
CONV_W = 4
FOURIER_W = D_MODEL // 4
FOURIER_GROUPS = 4
FOURIER_GW = FOURIER_W // FOURIER_GROUPS
MIX_W = ATTN_W + LRU_W + FOURIER_W
REST_W = 2 * LRU_W
IN_W = 3 * ATTN_W + REST_W + FOURIER_W
FFN_HIDDEN = -(-8 * D_MODEL // (3 * 256)) * 256
ROPE_BASE = 10000.0
RG_C = 8.0
EPS = 1e-6
N_MOD = 6
COND_ROWS = 8

F32 = jnp.float32
BF16 = jnp.bfloat16
V7X_VMEM_LIMIT_BYTES = 56 * 1024 * 1024
CONV_HALO = 8


def _cparams(*semantics, flags=None):
    return pltpu.CompilerParams(dimension_semantics=semantics,
                                vmem_limit_bytes=V7X_VMEM_LIMIT_BYTES, flags=flags)


def _resident(shape, index_map):
    return pl.BlockSpec(shape, index_map, pipeline_mode=pl.Buffered(1))


def _lambda_init(l):
    return 0.8 - 0.6 * math.exp(-0.3 * l)


def _rms(x, axis):
    return x * lax.rsqrt(jnp.mean(x * x, axis=axis, keepdims=True) + EPS)


def _mod_kernel(cond_ref, w_ref, b_ref, o_ref):
    s = cond_ref[...]
    s = s * jax.nn.sigmoid(s)
    o_ref[0] = jnp.dot(s, w_ref[0], precision=lax.Precision.HIGHEST,
                       preferred_element_type=F32) + b_ref[0]


def _modulation(cond, w_mod, b_mod):
    tn = 1536
    n_out = N_MOD * D_MODEL
    return pl.pallas_call(
        _mod_kernel,
        grid=(DEPTH, n_out // tn),
        in_specs=[pl.BlockSpec((COND_ROWS, D_MODEL), lambda l, j: (0, 0)),
                  pl.BlockSpec((1, D_MODEL, tn), lambda l, j: (l, 0, j)),
                  pl.BlockSpec((1, 1, tn), lambda l, j: (l, 0, j))],
        out_specs=pl.BlockSpec((1, COND_ROWS, tn), lambda l, j: (l, 0, j)),
        out_shape=jax.ShapeDtypeStruct((DEPTH, COND_ROWS, n_out), F32),
        compiler_params=_cparams("arbitrary", "arbitrary"),
        name="modulation",
    )(cond, w_mod, b_mod.reshape(DEPTH, 1, n_out))


INPROJ_TILE = 512
INPROJ_ROW_GROUP = 256


def _inproj_kernel(*refs, rope, kv_seq_len, n_alias):
    x_ref, mod_ref, g_ref, w_ref = refs[:4]
    refs = refs[4:]
    if rope:
        cos_ref, sin_lo_ref, sin_hi_ref = refs[:3]
        refs = refs[3:]
    refs = refs[n_alias:]
    qt_ref, k_ref, vt_ref, rest_ref, xf_ref = refs[:5]
    mod = mod_ref[0]
    tm = x_ref.shape[1]
    groups = [slice(r, r + INPROJ_ROW_GROUP) for r in range(0, tm, INPROJ_ROW_GROUP)]
    hs = [((_rms(x_ref[0, rows], -1) * g_ref[...]) * (1.0 + mod[1:2]) + mod[0:1]).astype(BF16)
          for rows in groups]
    projs = [jnp.dot(h, w_ref[...], preferred_element_type=F32) for h in hs]

    row = lax.broadcasted_iota(jnp.int32, (HEAD_W, INPROJ_ROW_GROUP), 0)
    scale = HEAD_DIM ** -0.5 * math.log2(math.e)
    for gi, (rows, proj) in enumerate(zip(groups, projs)):
        q = proj[:, :ATTN_W]
        k = proj[:, ATTN_W:2 * ATTN_W]
        v = proj[:, 2 * ATTN_W:3 * ATTN_W]
        if kv_seq_len is not None:
            assert kv_seq_len == INPROJ_ROW_GROUP
            kf_ref, vf_ref = refs[5:7]
            kf_ref[gi] = k
            vf_ref[gi] = v

        def rotary(t):
            return (t * cos_ref[rows] + pltpu.roll(t, HEAD_W - 16, 1) * sin_lo_ref[rows]
                    + pltpu.roll(t, 16, 1) * sin_hi_ref[rows])

        for hd in range(N_HEADS):
            qh = q[:, hd * HEAD_W:(hd + 1) * HEAD_W]
            kh = k[:, hd * HEAD_W:(hd + 1) * HEAD_W]
            if rope:
                qh = rotary(qh)
                kh = rotary(kh)
            qht = (qh * scale).T
            qt_ref[0, 2 * hd, :, rows] = jnp.where(row < HEAD_DIM, qht, 0.0).astype(BF16)
            qt_ref[0, 2 * hd + 1, :, rows] = jnp.where(row >= HEAD_DIM, qht, 0.0).astype(BF16)
            k_ref[0, rows, hd * HEAD_W:(hd + 1) * HEAD_W] = kh.astype(BF16)
        vt_ref[0, :, rows] = v.T.astype(BF16)
        rest_ref[0, rows] = proj[:, 3 * ATTN_W:3 * ATTN_W + REST_W]
        xf_ref[0, rows] = proj[:, 3 * ATTN_W + REST_W:].astype(BF16)


def _inproj(x, mod, goff, g_pre, w_in, rope_tabs, kv_out=None):
    G, T, _ = x.shape
    tm = INPROJ_TILE
    rope = rope_tabs is not None
    in_specs = [pl.BlockSpec((1, tm, D_MODEL), lambda g, i: (g, i, 0)),
                pl.BlockSpec((1, N_MOD, D_MODEL), lambda g, i: (g + goff, 0, 0)),
                pl.BlockSpec((1, D_MODEL), lambda g, i: (0, 0)),
                _resident((D_MODEL, IN_W), lambda g, i: (0, 0))]
    args = [x, mod, g_pre, w_in]
    if rope:
        in_specs += [pl.BlockSpec((tm, HEAD_W), lambda g, i: (i, 0))] * 3
        args += list(rope_tabs)
    out_specs = [pl.BlockSpec((1, 2 * N_HEADS, HEAD_W, tm), lambda g, i: (g, 0, 0, i)),
                 pl.BlockSpec((1, tm, ATTN_W), lambda g, i: (g, i, 0)),
                 pl.BlockSpec((1, ATTN_W, tm), lambda g, i: (g, 0, i)),
                 pl.BlockSpec((1, tm, REST_W), lambda g, i: (g, i, 0)),
                 pl.BlockSpec((1, tm, FOURIER_W), lambda g, i: (g, i, 0))]
    out_shape = [jax.ShapeDtypeStruct((G, 2 * N_HEADS, HEAD_W, T), BF16),
                 jax.ShapeDtypeStruct((G, T, ATTN_W), BF16),
                 jax.ShapeDtypeStruct((G, ATTN_W, T), BF16),
                 jax.ShapeDtypeStruct((G, T, REST_W), F32),
                 jax.ShapeDtypeStruct((G, T, FOURIER_W), BF16)]
    aliases = {}
    kv_seq_len = None
    n_alias = 0
    if kv_out is not None:
        assert G == 1
        layer, kv_seq_len, n_layers, bufs = kv_out
        per_tile = tm // kv_seq_len
        out_specs += [pl.BlockSpec((per_tile, None, kv_seq_len, ATTN_W),
                                   lambda g, i: (i, layer, 0, 0))] * 2
        out_shape += [jax.ShapeDtypeStruct((T // kv_seq_len, n_layers, kv_seq_len, ATTN_W), F32)] * 2
        if bufs is not None:
            n_alias = len(bufs)
            aliases = {len(args) + j: 5 + j for j in range(n_alias)}
            in_specs += [pl.BlockSpec(memory_space=pl.ANY)] * n_alias
            args += list(bufs)
    return pl.pallas_call(
        functools.partial(_inproj_kernel, rope=rope, kv_seq_len=kv_seq_len, n_alias=n_alias),
        grid=(G, T // tm),
        in_specs=in_specs, out_specs=out_specs, out_shape=out_shape,
        input_output_aliases=aliases,
        compiler_params=_cparams("arbitrary", "arbitrary"),
        name="inproj_rope" if rope else "inproj_ctx",
    )(*args)


QUERY_BLOCK = 256
KEY_CHUNK = 256
SUBLANES = 8
SUM_ACCUMULATORS = 2


ROW_ACCUMULATORS = 2


def _accumulate_rows(accs, x, op):
    for idx, r in enumerate(range(0, x.shape[0], SUBLANES)):
        j = idx % len(accs)
        part = x[r:r + SUBLANES]
        accs[j] = part if accs[j] is None else op(accs[j], part)


def _finish_rows(accs, op, reduce):
    total = accs[0]
    for a in accs[1:]:
        total = op(total, a)
    return reduce(total, axis=0, keepdims=True)


def _attn_kernel(*refs, has_cache, li, n_heads, n_sub, n_keys):
    qt_ref, k_ref, vt_ref = refs[:3]
    refs = refs[3:]
    if has_cache:
        ck_ref, cv_ref = refs[:2]
        refs = refs[2:]
    wl_ref, g_ref, o_ref = refs[:3]
    s_bufs = refs[3:5]
    acc_bufs = refs[5:7]
    n_self = k_ref.shape[1]
    if has_cache:
        kall, vtall = refs[7:9]

        @pl.when(pl.program_id(2) == 0)
        def _():
            kall[0:n_self] = k_ref[0]
            kall[n_self:n_keys] = ck_ref[0].astype(BF16)
            vtall[:, 0:n_self] = vt_ref[0]
            vtall[:, n_self:n_keys] = cv_ref[0].T.astype(BF16)

    def k_chunk(hd, c):
        rows = slice(c * KEY_CHUNK, (c + 1) * KEY_CHUNK)
        return kall[rows, :] if has_cache else k_ref[0, rows, hd * HEAD_W:(hd + 1) * HEAD_W]

    def vt_chunk(hd, c):
        cols = slice(c * KEY_CHUNK, (c + 1) * KEY_CHUNK)
        return vtall[:, cols] if has_cache else vt_ref[0, hd * HEAD_W:(hd + 1) * HEAD_W, cols]

    units = [(hd, sub, m) for hd in range(n_heads) for sub in range(n_sub) for m in range(2)]
    n_chunks = n_keys // KEY_CHUNK
    results = {}
    col_max = None
    for t in range(len(units) + 1):
        if t < len(units):
            hd, sub, m = units[t]
            qt = qt_ref[0, 2 * hd + m, :, sub * QUERY_BLOCK:(sub + 1) * QUERY_BLOCK]
            new_max = [None] * ROW_ACCUMULATORS
        if t > 0:
            hd_p = units[t - 1][0]
            mx = _finish_rows(col_max, jnp.maximum, jnp.max)
            den = [None] * SUM_ACCUMULATORS
            acc_ref = acc_bufs[(t - 1) % 2]
        for c in range(n_chunks):
            rows = slice(c * KEY_CHUNK, (c + 1) * KEY_CHUNK)
            if t < len(units):
                s = jnp.dot(k_chunk(hd, c), qt, preferred_element_type=F32)
                s_bufs[t % 2][rows, :] = s
                _accumulate_rows(new_max, s, jnp.maximum)
            if t > 0:
                e = jnp.exp2(s_bufs[(t - 1) % 2][rows, :] - mx)
                _accumulate_rows(den, e, jnp.add)
                pv = jnp.dot(vt_chunk(hd_p, c), e.astype(BF16), preferred_element_type=F32)
                if c == 0:
                    acc_ref[...] = pv
                else:
                    acc_ref[...] += pv
        if t > 0:
            results[units[t - 1]] = acc_ref[...] / _finish_rows(den, jnp.add, jnp.sum)
        col_max = new_max

    wl = wl_ref[...]
    lam = (jnp.exp(jnp.sum(wl[0:1] * wl[1:2], axis=-1, keepdims=True))
           - jnp.exp(jnp.sum(wl[2:3] * wl[3:4], axis=-1, keepdims=True)) + li)
    for hd in range(n_heads):
        for sub in range(n_sub):
            ot = results[(hd, sub, 0)] - lam * results[(hd, sub, 1)]
            y = (_rms(ot, 0) * g_ref[...]) * (1.0 - li)
            o_ref[0, sub * QUERY_BLOCK:(sub + 1) * QUERY_BLOCK,
                  hd * HEAD_W:(hd + 1) * HEAD_W] = y.T.astype(BF16)


def _attention(qt, k, vt, cache, w_lambda_l, g_subln_l, li, n_seq, seq_len, n_heads, n_sub):
    G, _, _, T = qt.shape
    per_g = T // seq_len
    tq = n_sub * QUERY_BLOCK
    nq = seq_len // tq
    has_cache = cache is not None
    n_keys = seq_len

    def gi(b):
        return b // per_g

    def si(b):
        return b % per_g

    hw = n_heads * HEAD_W
    in_specs = [pl.BlockSpec((1, 2 * n_heads, HEAD_W, tq), lambda b, h, i: (gi(b), h, 0, si(b) * nq + i)),
                pl.BlockSpec((1, seq_len, hw), lambda b, h, i: (gi(b), si(b), h)),
                pl.BlockSpec((1, hw, seq_len), lambda b, h, i: (gi(b), h, si(b)))]
    args = [qt, k, vt]
    if has_cache:
        assert n_heads == 1
        ck, cv, layer = cache
        past = ck.shape[2]
        n_keys = seq_len + past
        in_specs += [pl.BlockSpec((1, None, past, HEAD_W), lambda b, h, i: (b, layer, 0, h))] * 2
        args += [ck, cv]
    scratch = [pltpu.VMEM((n_keys, QUERY_BLOCK), F32)] * 2
    scratch += [pltpu.VMEM((HEAD_W, QUERY_BLOCK), F32)] * 2
    if has_cache:
        scratch += [pltpu.VMEM((n_keys, HEAD_W), BF16), pltpu.VMEM((HEAD_W, n_keys), BF16)]
    in_specs += [pl.BlockSpec((4, HEAD_DIM), lambda b, h, i: (0, 0)),
                 pl.BlockSpec((HEAD_W, 1), lambda b, h, i: (0, 0))]
    args += [w_lambda_l, g_subln_l.reshape(HEAD_W, 1)]
    return pl.pallas_call(
        functools.partial(_attn_kernel, has_cache=has_cache, li=li, n_heads=n_heads,
                          n_sub=n_sub, n_keys=n_keys),
        grid=(n_seq, N_HEADS // n_heads, nq),
        in_specs=in_specs,
        out_specs=pl.BlockSpec((1, tq, hw), lambda b, h, i: (gi(b), si(b) * nq + i, h)),
        out_shape=jax.ShapeDtypeStruct((G, T, ATTN_W), BF16),
        scratch_shapes=scratch,
        compiler_params=_cparams("arbitrary", "arbitrary", "arbitrary"),
        name="attn_latent" if has_cache else "attn_ctx",
    )(*args)


def _scan_chunk(a, b, carry, reverse):
    n, w = a.shape
    groups = n // SUBLANES
    a = a.reshape(groups, SUBLANES, w)
    b = b.reshape(groups, SUBLANES, w)
    sub = lax.broadcasted_iota(jnp.int32, (groups, SUBLANES, w), 1)
    d = 1
    while d < SUBLANES:
        if reverse:
            shift, valid = SUBLANES - d, sub < SUBLANES - d
        else:
            shift, valid = d, sub >= d
        a_prev = jnp.where(valid, pltpu.roll(a, shift, 1), 1.0)
        b_prev = jnp.where(valid, pltpu.roll(b, shift, 1), 0.0)
        b = a * b_prev + b
        a = a * a_prev
        d *= 2
    edge = 0 if reverse else SUBLANES - 1
    hs = [None] * groups
    for g in (range(groups - 1, -1, -1) if reverse else range(groups)):
        hs[g] = a[g] * carry + b[g]
        carry = hs[g][edge:edge + 1]
    return jnp.concatenate(hs, axis=0), carry


def _lru_kernel(xr_ref, gr_ref, cw_ref, cb_ref, w_ref, bias_ref, lam_ref, h0_ref,
                out_ref, hfin_ref, xpad, hf_scr, ab_scr, bb_scr, *, seq_len, tc):
    nch = seq_len // tc
    zeros_halo = jnp.zeros((CONV_HALO, LRU_W), F32)
    xpad[0:CONV_HALO] = zeros_halo
    xpad[seq_len + CONV_HALO:seq_len + 2 * CONV_HALO] = zeros_halo
    xpad[CONV_HALO:seq_len + CONV_HALO] = xr_ref[0]
    lam_p = lam_ref[...]
    neg = -lam_p
    softplus = jnp.maximum(neg, 0.0) + jnp.log1p(jnp.exp(-jnp.abs(neg)))
    coef = -RG_C * softplus
    cw = cw_ref[...]
    h0 = h0_ref[0]

    def decay_and_input(xc, z, direction):
        off = direction * 2 * LRU_W
        r = jax.nn.sigmoid(z[:, off:off + LRU_W])
        i = jax.nn.sigmoid(z[:, off + LRU_W:off + 2 * LRU_W])
        log_a = coef[direction:direction + 1] * r
        a = jnp.exp(log_a)
        b = jnp.sqrt(-jnp.tanh(log_a) * (1.0 + a * a)) * (i * xc)
        return a, b

    def forward_chunk(c, carry):
        start = pl.multiple_of(c * tc, tc)
        win = xpad[pl.ds(start, tc + 2 * CONV_HALO), :]
        lo = CONV_HALO - CONV_W // 2
        xc = win[lo:lo + tc] * cw[0:1]
        for j in range(1, CONV_W):
            xc = xc + win[lo + j:lo + j + tc] * cw[j:j + 1]
        xc = xc + cb_ref[...]
        z = jnp.dot(xc.astype(BF16), w_ref[...], preferred_element_type=F32) + bias_ref[...]
        a_f, b_f = decay_and_input(xc, z, 0)
        hf, carry = _scan_chunk(a_f, b_f, carry, reverse=False)
        hf_scr[pl.ds(start, tc), :] = hf
        a_b, b_b = decay_and_input(xc, z, 1)
        ab_scr[pl.ds(start, tc), :] = a_b
        bb_scr[pl.ds(start, tc), :] = b_b
        return carry

    def backward_chunk(j, carry):
        c = nch - 1 - j
        start = pl.multiple_of(c * tc, tc)
        hb, carry = _scan_chunk(ab_scr[pl.ds(start, tc), :], bb_scr[pl.ds(start, tc), :],
                                carry, reverse=True)
        gate = jax.nn.gelu(gr_ref[0, pl.ds(start, tc), :], approximate=True)
        out_ref[0, pl.ds(start, tc), :] = ((hf_scr[pl.ds(start, tc), :] + hb) * gate).astype(BF16)
        return carry

    if nch == 1:
        hf_fin = forward_chunk(0, h0[0:1])
        hb_fin = backward_chunk(0, h0[1:2])
    else:
        hf_fin = lax.fori_loop(0, nch, forward_chunk, h0[0:1])
        hb_fin = lax.fori_loop(0, nch, backward_chunk, h0[1:2])
    hfin_ref[0, 0:1] = hf_fin
    hfin_ref[0, 1:2] = hb_fin


def _rg_lru(rest, conv_w, conv_b, w_gates, b_gates, lam_p, h0):
    B, L, _ = rest.shape
    tc = min(L, 256)
    return pl.pallas_call(
        functools.partial(_lru_kernel, seq_len=L, tc=tc),
        grid=(B,),
        in_specs=[pl.BlockSpec((1, L, LRU_W), lambda b: (b, 0, 0)),
                  pl.BlockSpec((1, L, LRU_W), lambda b: (b, 0, 1)),
                  pl.BlockSpec((CONV_W, LRU_W), lambda b: (0, 0)),
                  pl.BlockSpec((1, LRU_W), lambda b: (0, 0)),
                  pl.BlockSpec((LRU_W, 4 * LRU_W), lambda b: (0, 0)),
                  pl.BlockSpec((1, 4 * LRU_W), lambda b: (0, 0)),
                  pl.BlockSpec((2, LRU_W), lambda b: (0, 0)),
                  pl.BlockSpec((1, 2, LRU_W), lambda b: (b, 0, 0))],
        out_specs=[pl.BlockSpec((1, L, LRU_W), lambda b: (b, 0, 0)),
                   pl.BlockSpec((1, 2, LRU_W), lambda b: (b, 0, 0))],
        out_shape=[jax.ShapeDtypeStruct((B, L, LRU_W), BF16),
                   jax.ShapeDtypeStruct((B, 2, LRU_W), F32)],
        scratch_shapes=[pltpu.VMEM((L + 2 * CONV_HALO, LRU_W), F32),
                        pltpu.VMEM((L, LRU_W), F32),
                        pltpu.VMEM((L, LRU_W), F32),
                        pltpu.VMEM((L, LRU_W), F32)],
        compiler_params=_cparams("arbitrary"),
        name="rg_lru",
    )(rest, rest, conv_w, conv_b.reshape(1, LRU_W), w_gates, b_gates, lam_p, h0)


def _chan_dft_kernel(x_ref, w_ref, yc_ref, ys_ref):
    y = jnp.dot(x_ref[0], w_ref[...], preferred_element_type=F32)
    yc_ref[...] = y[:, :FOURIER_W].astype(BF16)
    ys_ref[...] = y[:, FOURIER_W:].astype(BF16)


def _seq_dft_kernel(fc_ref, fs_ref, yc_ref, ys_ref, o_ref, *, scale):
    acc = jnp.dot(fc_ref[...], yc_ref[...], preferred_element_type=F32)
    acc = acc - jnp.dot(fs_ref[...], ys_ref[...], preferred_element_type=F32)
    o_ref[...] = (acc * scale).astype(BF16)


def _fourier_direct(xf, chan_dft, seq_cos, seq_sin):
    B, L, _ = xf.shape
    y_shape = jax.ShapeDtypeStruct((L, B * FOURIER_W), BF16)
    yc, ys = pl.pallas_call(
        _chan_dft_kernel,
        grid=(B,),
        in_specs=[pl.BlockSpec((1, L, FOURIER_W), lambda b: (b, 0, 0)),
                  pl.BlockSpec((FOURIER_W, 2 * FOURIER_W), lambda b: (0, 0))],
        out_specs=[pl.BlockSpec((L, FOURIER_W), lambda b: (0, b))] * 2,
        out_shape=[y_shape, y_shape],
        compiler_params=_cparams("arbitrary"),
        name="fourier_channels",
    )(xf, chan_dft)
    return pl.pallas_call(
        functools.partial(_seq_dft_kernel, scale=(L * FOURIER_GW) ** -0.5),
        grid=(B,),
        in_specs=[pl.BlockSpec((L, L), lambda b: (0, 0)),
                  pl.BlockSpec((L, L), lambda b: (0, 0)),
                  pl.BlockSpec((L, FOURIER_W), lambda b: (0, b)),
                  pl.BlockSpec((L, FOURIER_W), lambda b: (0, b))],
        out_specs=pl.BlockSpec((L, FOURIER_W), lambda b: (b, 0)),
        out_shape=jax.ShapeDtypeStruct((B * L, FOURIER_W), BF16),
        compiler_params=_cparams("arbitrary"),
        name="fourier_sequence",
    )(seq_cos, seq_sin, yc, ys)


FFT_RADIX = 64
FFT_COLS = 2048
FFT_K1_PER_STEP = 8


def _fft_stage1_kernel(x_ref, f_ref, tc_ref, ts_ref, o_ref):
    a = jnp.dot(f_ref[...], x_ref[0], preferred_element_type=F32)
    a_re, a_im = a[:FFT_RADIX], a[FFT_RADIX:]
    tc, ts = tc_ref[...], ts_ref[...]
    o_ref[0, 0] = (a_re * tc + a_im * ts).astype(BF16)
    o_ref[0, 1] = (a_im * tc - a_re * ts).astype(BF16)


def _fft_stage2_kernel(b_ref, f_ref, w_ref, o_ref, *, scale):
    zs = []
    for j in range(FFT_K1_PER_STEP):
        bj = jnp.concatenate([b_ref[0, 0, j], b_ref[0, 1, j]], axis=0)
        z = jnp.dot(f_ref[...], bj, preferred_element_type=F32)
        zs.append(jnp.concatenate([z[:FFT_RADIX], z[FFT_RADIX:]], axis=1))
    z_all = jnp.concatenate(zs, axis=0).astype(BF16)
    out = jnp.dot(z_all, w_ref[...], preferred_element_type=F32) * scale
    o_ref[0] = out.reshape(FFT_K1_PER_STEP, FFT_RADIX, FOURIER_W).astype(BF16)


def _fourier_fft(xf, stage1_dft, twiddle_cos, twiddle_sin, stage2_dft, chan_dft_stacked):
    B, L, _ = xf.shape
    R = FFT_RADIX
    assert L == R * R
    cols = R * FOURIER_W
    x2 = xf.reshape(B, R, cols)
    mid = pl.pallas_call(
        _fft_stage1_kernel,
        grid=(cols // FFT_COLS, B),
        in_specs=[pl.BlockSpec((1, R, FFT_COLS), lambda j, b: (b, 0, j)),
                  pl.BlockSpec((2 * R, R), lambda j, b: (0, 0)),
                  pl.BlockSpec((R, FFT_COLS), lambda j, b: (0, j)),
                  pl.BlockSpec((R, FFT_COLS), lambda j, b: (0, j))],
        out_specs=pl.BlockSpec((1, 2, R, FFT_COLS), lambda j, b: (b, 0, 0, j)),
        out_shape=jax.ShapeDtypeStruct((B, 2, R, cols), BF16),
        compiler_params=_cparams("arbitrary", "arbitrary"),
        name="fft_stage1",
    )(x2, stage1_dft, twiddle_cos, twiddle_sin)
    mid = mid.reshape(B, 2, R, R, FOURIER_W)
    out = pl.pallas_call(
        functools.partial(_fft_stage2_kernel, scale=(L * FOURIER_GW) ** -0.5),
        grid=(B, R // FFT_K1_PER_STEP),
        in_specs=[pl.BlockSpec((1, 2, FFT_K1_PER_STEP, R, FOURIER_W), lambda b, i: (b, 0, i, 0, 0)),
                  pl.BlockSpec((2 * R, 2 * R), lambda b, i: (0, 0)),
                  pl.BlockSpec((2 * FOURIER_W, FOURIER_W), lambda b, i: (0, 0))],
        out_specs=pl.BlockSpec((1, FFT_K1_PER_STEP, R, FOURIER_W), lambda b, i: (b, i, 0, 0)),
        out_shape=jax.ShapeDtypeStruct((B, R, R, FOURIER_W), BF16),
        compiler_params=_cparams("arbitrary", "arbitrary"),
        name="fft_stage2",
    )(mid, stage2_dft, chan_dft_stacked)
    return out.transpose(0, 2, 1, 3).reshape(B, L, FOURIER_W)


POST_TILE = 512
POST_ROW_GROUP = 256
def _post_kernel(x_ref, attn_ref, lru_ref, four_ref, mod_ref, gmix_ref, gpre_ref, gffn_ref,
                 wo_ref, wgu_ref, wd_ref, o_ref):
    mod = mod_ref[0]
    tm = x_ref.shape[1]
    groups = [slice(r, r + POST_ROW_GROUP) for r in range(0, tm, POST_ROW_GROUP)]

    def mix_out(rows):
        o = jnp.dot(attn_ref[0, rows], wo_ref[0:ATTN_W], preferred_element_type=F32)
        o = o + jnp.dot(lru_ref[0, rows], wo_ref[ATTN_W:ATTN_W + LRU_W], preferred_element_type=F32)
        return o + jnp.dot(four_ref[0, rows], wo_ref[ATTN_W + LRU_W:MIX_W], preferred_element_type=F32)

    def residual_and_prenorm(rows, o):
        x1 = x_ref[0, rows] + mod[2:3] * (_rms(o, -1) * gmix_ref[...])
        h = (_rms(x1, -1) * gpre_ref[...]) * (1.0 + mod[4:5]) + mod[3:4]
        return x1, h.astype(BF16)

    def gate_up(hb):
        gt = jnp.dot(hb, wgu_ref[:, :FFN_HIDDEN], preferred_element_type=F32)
        up = jnp.dot(hb, wgu_ref[:, FFN_HIDDEN:], preferred_element_type=F32)
        return ((gt * jax.nn.sigmoid(gt)) * up).astype(BF16)

    mixed = [mix_out(rows) for rows in groups]
    normed = [residual_and_prenorm(rows, o) for rows, o in zip(groups, mixed)]
    acts = [gate_up(hb) for _, hb in normed]
    downs = [jnp.dot(act, wd_ref[...], preferred_element_type=F32) for act in acts]
    for rows, (x1, _), f in zip(groups, normed, downs):
        o_ref[0, rows] = x1 + mod[5:6] * (_rms(f, -1) * gffn_ref[...])


def _post(x, attn, lru, four, mod, goff, g_post_mix, g_pre_ffn, g_post_ffn,
          w_out, w_gate_up, w_down, tm):
    G, T, _ = x.shape
    vec = pl.BlockSpec((1, D_MODEL), lambda g, i: (0, 0))
    return pl.pallas_call(
        _post_kernel,
        grid=(G, T // tm),
        in_specs=[pl.BlockSpec((1, tm, D_MODEL), lambda g, i: (g, i, 0)),
                  pl.BlockSpec((1, tm, ATTN_W), lambda g, i: (g, i, 0)),
                  pl.BlockSpec((1, tm, LRU_W), lambda g, i: (g, i, 0)),
                  pl.BlockSpec((1, tm, FOURIER_W), lambda g, i: (g, i, 0)),
                  pl.BlockSpec((1, N_MOD, D_MODEL), lambda g, i: (g + goff, 0, 0)),
                  vec, vec, vec,
                  _resident((MIX_W, D_MODEL), lambda g, i: (0, 0)),
                  _resident((D_MODEL, 2 * FFN_HIDDEN), lambda g, i: (0, 0)),
                  _resident((FFN_HIDDEN, D_MODEL), lambda g, i: (0, 0))],
        out_specs=pl.BlockSpec((1, tm, D_MODEL), lambda g, i: (g, i, 0)),
        out_shape=jax.ShapeDtypeStruct((G, T, D_MODEL), F32),
        compiler_params=_cparams("arbitrary", "arbitrary"),
        name="post",
    )(x, attn, lru, four, mod, g_post_mix, g_pre_ffn, g_post_ffn, w_out, w_gate_up, w_down)


def _rope_tables(n_tokens):
    rows = n_tokens // GRID_W
    row = jnp.repeat(jnp.arange(rows, dtype=F32), GRID_W)
    col = jnp.tile(jnp.arange(GRID_W, dtype=F32), rows)
    n = HEAD_DIM // 4
    inv = ROPE_BASE ** (-jnp.arange(n, dtype=F32) / n)
    ang_r, ang_c = row[:, None] * inv, col[:, None] * inv
    zero = jnp.zeros_like(ang_r)
    cos64 = jnp.concatenate([jnp.cos(ang_r)] * 2 + [jnp.cos(ang_c)] * 2, axis=-1)
    sin_lo = jnp.concatenate([-jnp.sin(ang_r), zero, -jnp.sin(ang_c), zero], axis=-1)
    sin_hi = jnp.concatenate([zero, jnp.sin(ang_r), zero, jnp.sin(ang_c)], axis=-1)
    return tuple(jnp.tile(t, (1, 2)) for t in (cos64, sin_lo, sin_hi))


def _seq_dft_tables(n):
    r = 1 << (n.bit_length() // 2)
    l = jnp.arange(n, dtype=jnp.int32)

    def cos_sin(rows, period):
        k = jnp.arange(rows, dtype=jnp.int32)
        ang = ((k[:, None] * l[None, :]) % period).astype(F32) * (2.0 * math.pi / period)
        return jnp.cos(ang), jnp.sin(ang)

    ca, sa = cos_sin(n // r, n // r)
    cb, sb = cos_sin(r, n)
    cos = ca[:, None, :] * cb[None] - sa[:, None, :] * sb[None]
    sin = sa[:, None, :] * cb[None] + ca[:, None, :] * sb[None]
    return cos.reshape(n, n).astype(BF16), sin.reshape(n, n).astype(BF16)


def _chan_dft_table():
    k = np.arange(FOURIER_GW)
    ang = 2.0 * np.pi * ((k[:, None] * k[None, :]) % FOURIER_GW) / FOURIER_GW
    eye = np.eye(FOURIER_GROUPS)
    table = np.concatenate([np.kron(eye, np.cos(ang)), np.kron(eye, np.sin(ang))], axis=1)
    return jnp.asarray(table, dtype=F32).astype(BF16)


def _fft_tables(n):
    r = FFT_RADIX
    k = np.arange(r)
    ang = 2.0 * np.pi * ((k[:, None] * k[None, :]) % r) / r
    cm, sm = np.cos(ang), np.sin(ang)
    stage1 = np.concatenate([cm, -sm], axis=0)
    stage2 = np.block([[cm, sm], [-sm, cm]])
    kj = jnp.arange(r, dtype=jnp.int32)
    tw = (kj[:, None] * kj[None, :]).astype(F32) * (2.0 * math.pi / n)
    expand = lambda t: jnp.repeat(t, FOURIER_W, axis=1)
    return (jnp.asarray(stage1, dtype=F32).astype(BF16), expand(jnp.cos(tw)), expand(jnp.sin(tw)),
            jnp.asarray(stage2, dtype=F32).astype(BF16))


def _block_diag(w):
    n, bw, _ = w.shape
    eye = jnp.eye(n, dtype=w.dtype)
    return (eye[:, None, :, None] * w[:, :, None, :]).reshape(n * bw, n * bw)


def kernel(x_prompt, x_sample, cache_k, cache_v, state_lru, c, c_ctx, w_mod, b_mod, g_pre_mix, g_post_mix, g_pre_ffn, g_post_ffn, w_in, w_out, w_lambda, g_subln, conv_w, conv_b, lru_wa, lru_ba, lru_wx, lru_bx, lru_lambda, w_gate_up, w_down):
    n_ctx, ctx_len, _ = x_prompt.shape
    n_lat, lat_len, _ = x_sample.shape
    past = cache_k.shape[2]

    cond = jnp.concatenate(
        [c_ctx[None, :], c, jnp.zeros((COND_ROWS - 1 - n_lat, D_MODEL), F32)], axis=0)
    mod_all = _modulation(cond, w_mod, b_mod).reshape(DEPTH, COND_ROWS, N_MOD, D_MODEL)

    rope_tabs = _rope_tables(lat_len)
    chan_dft = _chan_dft_table()
    chan_dft_stacked = jnp.concatenate([chan_dft[:, :FOURIER_W], chan_dft[:, FOURIER_W:]], axis=0)
    dft_ctx = _seq_dft_tables(ctx_len)
    fft_lat = _fft_tables(lat_len)
    ck = cache_k.reshape(n_lat, DEPTH, past, ATTN_W)
    cv = cache_v.reshape(n_lat, DEPTH, past, ATTN_W)
    h0_ctx = jnp.zeros((n_ctx, 2, LRU_W), F32)

    xp = x_prompt.reshape(1, n_ctx * ctx_len, D_MODEL)
    xs = x_sample
    kv_bufs, new_h = None, []
    for l in range(DEPTH):
        li = _lambda_init(l)
        mod = mod_all[l]
        w_in_l = w_in[l].astype(BF16)
        w_out_l = w_out[l].astype(BF16)
        w_gu_l = w_gate_up[l].astype(BF16)
        w_down_l = w_down[l].astype(BF16)
        w_gates = jnp.concatenate(
            [_block_diag(lru_wa[l, 0]), _block_diag(lru_wx[l, 0]),
             _block_diag(lru_wa[l, 1]), _block_diag(lru_wx[l, 1])], axis=1).astype(BF16)
        b_gates = jnp.concatenate(
            [lru_ba[l, 0], lru_bx[l, 0], lru_ba[l, 1], lru_bx[l, 1]])[None, :]
        vecs = [g[l][None, :] for g in (g_post_mix, g_pre_ffn, g_post_ffn)]

        qt, k, vt, rest, xf, *kv_bufs = _inproj(
            xp, mod, 0, g_pre_mix[l][None, :], w_in_l, None,
            kv_out=(l, ctx_len, DEPTH, kv_bufs))
        attn = _attention(qt, k, vt, None, w_lambda[l], g_subln[l], li,
                          n_seq=n_ctx, seq_len=ctx_len, n_heads=N_HEADS,
                          n_sub=ctx_len // QUERY_BLOCK)
        rest_seq = rest.reshape(n_ctx, ctx_len, REST_W)
        lru, h_fin = _rg_lru(rest_seq, conv_w[l], conv_b[l], w_gates, b_gates,
                             lru_lambda[l], h0_ctx)
        new_h.append(h_fin)
        four = _fourier_direct(xf.reshape(n_ctx, ctx_len, FOURIER_W), chan_dft, *dft_ctx)
        xp = _post(xp, attn, lru.reshape(1, n_ctx * ctx_len, LRU_W),
                   four.reshape(1, n_ctx * ctx_len, FOURIER_W), mod, 0,
                   *vecs, w_out_l, w_gu_l, w_down_l, tm=POST_TILE)

        qt, k, vt, rest, xf = _inproj(
            xs, mod, 1, g_pre_mix[l][None, :], w_in_l, rope_tabs)
        attn = _attention(qt, k, vt, (ck, cv, l), w_lambda[l], g_subln[l], li,
                          n_seq=n_lat, seq_len=lat_len, n_heads=1, n_sub=8)
        lru, _ = _rg_lru(rest, conv_w[l], conv_b[l], w_gates, b_gates,
                         lru_lambda[l], state_lru[:, l])
        four = _fourier_fft(xf, fft_lat[0], fft_lat[1], fft_lat[2], fft_lat[3], chan_dft_stacked)
        xs = _post(xs, attn, lru, four, mod, 1,
                   *vecs, w_out_l, w_gu_l, w_down_l, tm=POST_TILE)

    return (xp.reshape(n_ctx, ctx_len, D_MODEL), xs,
            kv_bufs[0].reshape(n_ctx, DEPTH, ctx_len, N_HEADS, 2, HEAD_DIM),
            kv_bufs[1].reshape(n_ctx, DEPTH, ctx_len, N_HEADS, HEAD_W), jnp.stack(new_h, axis=1))
```

```python
import functools
import math

import jax
import jax.numpy as jnp
import numpy as np
from jax import lax
from jax.experimental import pallas as pl
from jax.experimental.pallas import tpu as pltpu

D_MODEL = 1024
DEPTH = 2
GRID_W = 64
HEAD_DIM = 64
N_HEADS = 4
HEAD_W = 2 * HEAD_DIM
ATTN_W = N_HEADS * HEAD_W
LRU_W = D_MODEL // 4
LRU_BLOCKS = 4
CONV_W = 4
FOURIER_W = D_MODEL // 4
FOURIER_GROUPS = 4
FOURIER_GW = FOURIER_W // FOURIER_GROUPS
MIX_W = ATTN_W + LRU_W + FOURIER_W
REST_W = 2 * LRU_W
IN_W = 3 * ATTN_W + REST_W + FOURIER_W
FFN_HIDDEN = -(-8 * D_MODEL // (3 * 256)) * 256
ROPE_BASE = 10000.0
RG_C = 8.0
EPS = 1e-6
N_MOD = 6
COND_ROWS = 8

F32 = jnp.float32
BF16 = jnp.bfloat16
V7X_VMEM_LIMIT_BYTES = 56 * 1024 * 1024
CONV_HALO = 8


def _cparams(*semantics, flags=None):
    return pltpu.CompilerParams(dimension_semantics=semantics,
                                vmem_limit_bytes=V7X_VMEM_LIMIT_BYTES, flags=flags)


def _resident(shape, index_map):
    return pl.BlockSpec(shape, index_map, pipeline_mode=pl.Buffered(1))


def _lambda_init(l):
    return 0.8 - 0.6 * math.exp(-0.3 * l)


def _rms(x, axis):
    return x * lax.rsqrt(jnp.mean(x * x, axis=axis, keepdims=True) + EPS)


def _mod_kernel(cond_ref, w_ref, b_ref, o_ref):
    s = cond_ref[...]
    s = s * jax.nn.sigmoid(s)
    o_ref[0] = jnp.dot(s, w_ref[0], precision=lax.Precision.HIGHEST,
                       preferred_element_type=F32) + b_ref[0]


def _modulation(cond, w_mod, b_mod):
    tn = 1536
    n_out = N_MOD * D_MODEL
    return pl.pallas_call(
        _mod_kernel,
        grid=(DEPTH, n_out // tn),
        in_specs=[pl.BlockSpec((COND_ROWS, D_MODEL), lambda l, j: (0, 0)),
                  pl.BlockSpec((1, D_MODEL, tn), lambda l, j: (l, 0, j)),
                  pl.BlockSpec((1, 1, tn), lambda l, j: (l, 0, j))],
        out_specs=pl.BlockSpec((1, COND_ROWS, tn), lambda l, j: (l, 0, j)),
        out_shape=jax.ShapeDtypeStruct((DEPTH, COND_ROWS, n_out), F32),
        compiler_params=_cparams("arbitrary", "arbitrary"),
        name="modulation",
    )(cond, w_mod, b_mod.reshape(DEPTH, 1, n_out))


INPROJ_TILE = 512
INPROJ_ROW_GROUP = 256


def _inproj_kernel(*refs, rope, kv_seq_len, n_alias):
    x_ref, mod_ref, g_ref, w_ref = refs[:4]
    refs = refs[4:]
    if rope:
        cos_ref, sin_lo_ref, sin_hi_ref = refs[:3]
        refs = refs[3:]
    refs = refs[n_alias:]
    qt_ref, k_ref, vt_ref, rest_ref, xf_ref = refs[:5]
    mod = mod_ref[0]
    tm = x_ref.shape[1]
    groups = [slice(r, r + INPROJ_ROW_GROUP) for r in range(0, tm, INPROJ_ROW_GROUP)]
    hs = [((_rms(x_ref[0, rows], -1) * g_ref[...]) * (1.0 + mod[1:2]) + mod[0:1]).astype(BF16)
          for rows in groups]
    projs = [jnp.dot(h, w_ref[...], preferred_element_type=F32) for h in hs]

    row = lax.broadcasted_iota(jnp.int32, (HEAD_W, INPROJ_ROW_GROUP), 0)
    scale = HEAD_DIM ** -0.5 * math.log2(math.e)
    for gi, (rows, proj) in enumerate(zip(groups, projs)):
        q = proj[:, :ATTN_W]
        k = proj[:, ATTN_W:2 * ATTN_W]
        v = proj[:, 2 * ATTN_W:3 * ATTN_W]
        if kv_seq_len is not None:
            assert kv_seq_len == INPROJ_ROW_GROUP
            kf_ref, vf_ref = refs[5:7]
            kf_ref[gi] = k
            vf_ref[gi] = v

        def rotary(t):
            return (t * cos_ref[rows] + pltpu.roll(t, HEAD_W - 16, 1) * sin_lo_ref[rows]
                    + pltpu.roll(t, 16, 1) * sin_hi_ref[rows])

        for hd in range(N_HEADS):
            qh = q[:, hd * HEAD_W:(hd + 1) * HEAD_W]
            kh = k[:, hd * HEAD_W:(hd + 1) * HEAD_W]
            if rope:
                qh = rotary(qh)
                kh = rotary(kh)
            qht = (qh * scale).T
            qt_ref[0, 2 * hd, :, rows] = jnp.where(row < HEAD_DIM, qht, 0.0).astype(BF16)
            qt_ref[0, 2 * hd + 1, :, rows] = jnp.where(row >= HEAD_DIM, qht, 0.0).astype(BF16)
            k_ref[0, rows, hd * HEAD_W:(hd + 1) * HEAD_W] = kh.astype(BF16)
        vt_ref[0, :, rows] = v.T.astype(BF16)
        rest_ref[0, rows] = proj[:, 3 * ATTN_W:3 * ATTN_W + REST_W]
        xf_ref[0, rows] = proj[:, 3 * ATTN_W + REST_W:].astype(BF16)


def _inproj(x, mod, goff, g_pre, w_in, rope_tabs, kv_out=None):
    G, T, _ = x.shape
    tm = INPROJ_TILE
    rope = rope_tabs is not None
    in_specs = [pl.BlockSpec((1, tm, D_MODEL), lambda g, i: (g, i, 0)),
                pl.BlockSpec((1, N_MOD, D_MODEL), lambda g, i: (g + goff, 0, 0)),
                pl.BlockSpec((1, D_MODEL), lambda g, i: (0, 0)),
                _resident((D_MODEL, IN_W), lambda g, i: (0, 0))]
    args = [x, mod, g_pre, w_in]
    if rope:
        in_specs += [pl.BlockSpec((tm, HEAD_W), lambda g, i: (i, 0))] * 3
        args += list(rope_tabs)
    out_specs = [pl.BlockSpec((1, 2 * N_HEADS, HEAD_W, tm), lambda g, i: (g, 0, 0, i)),
                 pl.BlockSpec((1, tm, ATTN_W), lambda g, i: (g, i, 0)),
                 pl.BlockSpec((1, ATTN_W, tm), lambda g, i: (g, 0, i)),
                 pl.BlockSpec((1, tm, REST_W), lambda g, i: (g, i, 0)),
                 pl.BlockSpec((1, tm, FOURIER_W), lambda g, i: (g, i, 0))]
    out_shape = [jax.ShapeDtypeStruct((G, 2 * N_HEADS, HEAD_W, T), BF16),
                 jax.ShapeDtypeStruct((G, T, ATTN_W), BF16),
                 jax.ShapeDtypeStruct((G, ATTN_W, T), BF16),
                 jax.ShapeDtypeStruct((G, T, REST_W), F32),
                 jax.ShapeDtypeStruct((G, T, FOURIER_W), BF16)]
    aliases = {}
    kv_seq_len = None
    n_alias = 0
    if kv_out is not None:
        assert G == 1
        layer, kv_seq_len, bufs = kv_out
        per_tile = tm // kv_seq_len
        out_specs += [pl.BlockSpec((per_tile, None, kv_seq_len, ATTN_W),
                                   lambda g, i: (i, layer, 0, 0))] * 2
        out_shape += [jax.ShapeDtypeStruct(b.shape, b.dtype) for b in bufs]
        n_alias = len(bufs)
        aliases = {len(args) + j: 5 + j for j in range(n_alias)}
        in_specs += [pl.BlockSpec(memory_space=pl.ANY)] * n_alias
        args += list(bufs)
    return pl.pallas_call(
        functools.partial(_inproj_kernel, rope=rope, kv_seq_len=kv_seq_len, n_alias=n_alias),
        grid=(G, T // tm),
        in_specs=in_specs, out_specs=out_specs, out_shape=out_shape,
        input_output_aliases=aliases,
        compiler_params=_cparams("arbitrary", "arbitrary"),
        name="inproj_rope" if rope else "inproj_ctx",
    )(*args)


QUERY_BLOCK = 256
KEY_CHUNK = 256
SUBLANES = 8
SUM_ACCUMULATORS = 2


ROW_ACCUMULATORS = 2


def _accumulate_rows(accs, x, op):
    for idx, r in enumerate(range(0, x.shape[0], SUBLANES)):
        j = idx % len(accs)
        part = x[r:r + SUBLANES]
        accs[j] = part if accs[j] is None else op(accs[j], part)


def _finish_rows(accs, op, reduce):
    total = accs[0]
    for a in accs[1:]:
        total = op(total, a)
    return reduce(total, axis=0, keepdims=True)


def _attn_kernel(*refs, has_cache, li, n_heads, n_sub, n_keys):
    qt_ref, k_ref, vt_ref = refs[:3]
    refs = refs[3:]
    if has_cache:
        ck_ref, cv_ref = refs[:2]
        refs = refs[2:]
    wl_ref, g_ref, o_ref = refs[:3]
    s_bufs = refs[3:5]
    acc_bufs = refs[5:7]
    n_self = k_ref.shape[1]
    if has_cache:
        kall, vtall = refs[7:9]

        @pl.when(pl.program_id(2) == 0)
        def _():
            kall[0:n_self] = k_ref[0]
            kall[n_self:n_keys] = ck_ref[0].astype(BF16)
            vtall[:, 0:n_self] = vt_ref[0]
            vtall[:, n_self:n_keys] = cv_ref[0].T.astype(BF16)

    def k_chunk(hd, c):
        rows = slice(c * KEY_CHUNK, (c + 1) * KEY_CHUNK)
        return kall[rows, :] if has_cache else k_ref[0, rows, hd * HEAD_W:(hd + 1) * HEAD_W]

    def vt_chunk(hd, c):
        cols = slice(c * KEY_CHUNK, (c + 1) * KEY_CHUNK)
        return vtall[:, cols] if has_cache else vt_ref[0, hd * HEAD_W:(hd + 1) * HEAD_W, cols]

    units = [(hd, sub, m) for hd in range(n_heads) for sub in range(n_sub) for m in range(2)]
    n_chunks = n_keys // KEY_CHUNK
    results = {}
    col_max = None
    for t in range(len(units) + 1):
        if t < len(units):
            hd, sub, m = units[t]
            qt = qt_ref[0, 2 * hd + m, :, sub * QUERY_BLOCK:(sub + 1) * QUERY_BLOCK]
            new_max = [None] * ROW_ACCUMULATORS
        if t > 0:
            hd_p = units[t - 1][0]
            mx = _finish_rows(col_max, jnp.maximum, jnp.max)
            den = [None] * SUM_ACCUMULATORS
            acc_ref = acc_bufs[(t - 1) % 2]
        for c in range(n_chunks):
            rows = slice(c * KEY_CHUNK, (c + 1) * KEY_CHUNK)
            if t < len(units):
                s = jnp.dot(k_chunk(hd, c), qt, preferred_element_type=F32)
                s_bufs[t % 2][rows, :] = s
                _accumulate_rows(new_max, s, jnp.maximum)
            if t > 0:
                e = jnp.exp2(s_bufs[(t - 1) % 2][rows, :] - mx)
                _accumulate_rows(den, e, jnp.add)
                pv = jnp.dot(vt_chunk(hd_p, c), e.astype(BF16), preferred_element_type=F32)
                if c == 0:
                    acc_ref[...] = pv
                else:
                    acc_ref[...] += pv
        if t > 0:
            results[units[t - 1]] = acc_ref[...] / _finish_rows(den, jnp.add, jnp.sum)
        col_max = new_max

    wl = wl_ref[...]
    lam = (jnp.exp(jnp.sum(wl[0:1] * wl[1:2], axis=-1, keepdims=True))
           - jnp.exp(jnp.sum(wl[2:3] * wl[3:4], axis=-1, keepdims=True)) + li)
    for hd in range(n_heads):
        for sub in range(n_sub):
            ot = results[(hd, sub, 0)] - lam * results[(hd, sub, 1)]
            y = (_rms(ot, 0) * g_ref[...]) * (1.0 - li)
            o_ref[0, sub * QUERY_BLOCK:(sub + 1) * QUERY_BLOCK,
                  hd * HEAD_W:(hd + 1) * HEAD_W] = y.T.astype(BF16)


def _attention(qt, k, vt, cache, w_lambda_l, g_subln_l, li, n_seq, seq_len, n_heads, n_sub):
    G, _, _, T = qt.shape
    per_g = T // seq_len
    tq = n_sub * QUERY_BLOCK
    nq = seq_len // tq
    has_cache = cache is not None
    n_keys = seq_len

    def gi(b):
        return b // per_g

    def si(b):
        return b % per_g

    hw = n_heads * HEAD_W
    in_specs = [pl.BlockSpec((1, 2 * n_heads, HEAD_W, tq), lambda b, h, i: (gi(b), h, 0, si(b) * nq + i)),
                pl.BlockSpec((1, seq_len, hw), lambda b, h, i: (gi(b), si(b), h)),
                pl.BlockSpec((1, hw, seq_len), lambda b, h, i: (gi(b), h, si(b)))]
    args = [qt, k, vt]
    if has_cache:
        assert n_heads == 1
        ck, cv, layer = cache
        past = ck.shape[2]
        n_keys = seq_len + past
        in_specs += [pl.BlockSpec((1, None, past, HEAD_W), lambda b, h, i: (b, layer, 0, h))] * 2
        args += [ck, cv]
    scratch = [pltpu.VMEM((n_keys, QUERY_BLOCK), F32)] * 2
    scratch += [pltpu.VMEM((HEAD_W, QUERY_BLOCK), F32)] * 2
    if has_cache:
        scratch += [pltpu.VMEM((n_keys, HEAD_W), BF16), pltpu.VMEM((HEAD_W, n_keys), BF16)]
    in_specs += [pl.BlockSpec((4, HEAD_DIM), lambda b, h, i: (0, 0)),
                 pl.BlockSpec((HEAD_W, 1), lambda b, h, i: (0, 0))]
    args += [w_lambda_l, g_subln_l.reshape(HEAD_W, 1)]
    return pl.pallas_call(
        functools.partial(_attn_kernel, has_cache=has_cache, li=li, n_heads=n_heads,
                          n_sub=n_sub, n_keys=n_keys),
        grid=(n_seq, N_HEADS // n_heads, nq),
        in_specs=in_specs,
        out_specs=pl.BlockSpec((1, tq, hw), lambda b, h, i: (gi(b), si(b) * nq + i, h)),
        out_shape=jax.ShapeDtypeStruct((G, T, ATTN_W), BF16),
        scratch_shapes=scratch,
        compiler_params=_cparams("arbitrary", "arbitrary", "arbitrary"),
        name="attn_latent" if has_cache else "attn_ctx",
    )(*args)


def _scan_chunk(a, b, carry, reverse):
    n, w = a.shape
    groups = n // SUBLANES
    a = a.reshape(groups, SUBLANES, w)
    b = b.reshape(groups, SUBLANES, w)
    sub = lax.broadcasted_iota(jnp.int32, (groups, SUBLANES, w), 1)
    d = 1
    while d < SUBLANES:
        if reverse:
            shift, valid = SUBLANES - d, sub < SUBLANES - d
        else:
            shift, valid = d, sub >= d
        a_prev = jnp.where(valid, pltpu.roll(a, shift, 1), 1.0)
        b_prev = jnp.where(valid, pltpu.roll(b, shift, 1), 0.0)
        b = a * b_prev + b
        a = a * a_prev
        d *= 2
    edge = 0 if reverse else SUBLANES - 1
    hs = [None] * groups
    for g in (range(groups - 1, -1, -1) if reverse else range(groups)):
        hs[g] = a[g] * carry + b[g]
        carry = hs[g][edge:edge + 1]
    return jnp.concatenate(hs, axis=0), carry


def _lru_kernel(xr_ref, gr_ref, cw_ref, cb_ref, w_ref, bias_ref, lam_ref, h0_ref,
                out_ref, hfin_ref, xpad, hf_scr, ab_scr, bb_scr, *, seq_len, tc):
    nch = seq_len // tc
    zeros_halo = jnp.zeros((CONV_HALO, LRU_W), F32)
    xpad[0:CONV_HALO] = zeros_halo
    xpad[seq_len + CONV_HALO:seq_len + 2 * CONV_HALO] = zeros_halo
    xpad[CONV_HALO:seq_len + CONV_HALO] = xr_ref[0]
    lam_p = lam_ref[...]
    neg = -lam_p
    softplus = jnp.maximum(neg, 0.0) + jnp.log1p(jnp.exp(-jnp.abs(neg)))
    coef = -RG_C * softplus
    cw = cw_ref[...]
    h0 = h0_ref[0]

    def decay_and_input(xc, z, direction):
        off = direction * 2 * LRU_W
        r = jax.nn.sigmoid(z[:, off:off + LRU_W])
        i = jax.nn.sigmoid(z[:, off + LRU_W:off + 2 * LRU_W])
        log_a = coef[direction:direction + 1] * r
        a = jnp.exp(log_a)
        b = jnp.sqrt(-jnp.tanh(log_a) * (1.0 + a * a)) * (i * xc)
        return a, b

    def forward_chunk(c, carry):
        start = pl.multiple_of(c * tc, tc)
        win = xpad[pl.ds(start, tc + 2 * CONV_HALO), :]
        lo = CONV_HALO - CONV_W // 2
        xc = win[lo:lo + tc] * cw[0:1]
        for j in range(1, CONV_W):
            xc = xc + win[lo + j:lo + j + tc] * cw[j:j + 1]
        xc = xc + cb_ref[...]
        z = jnp.dot(xc.astype(BF16), w_ref[...], preferred_element_type=F32) + bias_ref[...]
        a_f, b_f = decay_and_input(xc, z, 0)
        hf, carry = _scan_chunk(a_f, b_f, carry, reverse=False)
        hf_scr[pl.ds(start, tc), :] = hf
        a_b, b_b = decay_and_input(xc, z, 1)
        ab_scr[pl.ds(start, tc), :] = a_b
        bb_scr[pl.ds(start, tc), :] = b_b
        return carry

    def backward_chunk(j, carry):
        c = nch - 1 - j
        start = pl.multiple_of(c * tc, tc)
        hb, carry = _scan_chunk(ab_scr[pl.ds(start, tc), :], bb_scr[pl.ds(start, tc), :],
                                carry, reverse=True)
        gate = jax.nn.gelu(gr_ref[0, pl.ds(start, tc), :], approximate=True)
        out_ref[0, pl.ds(start, tc), :] = ((hf_scr[pl.ds(start, tc), :] + hb) * gate).astype(BF16)
        return carry

    if nch == 1:
        hf_fin = forward_chunk(0, h0[0:1])
        hb_fin = backward_chunk(0, h0[1:2])
    else:
        hf_fin = lax.fori_loop(0, nch, forward_chunk, h0[0:1])
        hb_fin = lax.fori_loop(0, nch, backward_chunk, h0[1:2])
    hfin_ref[0, 0:1] = hf_fin
    hfin_ref[0, 1:2] = hb_fin


def _rg_lru(rest, conv_w, conv_b, w_gates, b_gates, lam_p, h0):
    B, L, _ = rest.shape
    tc = min(L, 256)
    return pl.pallas_call(
        functools.partial(_lru_kernel, seq_len=L, tc=tc),
        grid=(B,),
        in_specs=[pl.BlockSpec((1, L, LRU_W), lambda b: (b, 0, 0)),
                  pl.BlockSpec((1, L, LRU_W), lambda b: (b, 0, 1)),
                  pl.BlockSpec((CONV_W, LRU_W), lambda b: (0, 0)),
                  pl.BlockSpec((1, LRU_W), lambda b: (0, 0)),
                  pl.BlockSpec((LRU_W, 4 * LRU_W), lambda b: (0, 0)),
                  pl.BlockSpec((1, 4 * LRU_W), lambda b: (0, 0)),
                  pl.BlockSpec((2, LRU_W), lambda b: (0, 0)),
                  pl.BlockSpec((1, 2, LRU_W), lambda b: (b, 0, 0))],
        out_specs=[pl.BlockSpec((1, L, LRU_W), lambda b: (b, 0, 0)),
                   pl.BlockSpec((1, 2, LRU_W), lambda b: (b, 0, 0))],
        out_shape=[jax.ShapeDtypeStruct((B, L, LRU_W), BF16),
                   jax.ShapeDtypeStruct((B, 2, LRU_W), F32)],
        scratch_shapes=[pltpu.VMEM((L + 2 * CONV_HALO, LRU_W), F32),
                        pltpu.VMEM((L, LRU_W), F32),
                        pltpu.VMEM((L, LRU_W), F32),
                        pltpu.VMEM((L, LRU_W), F32)],
        compiler_params=_cparams("arbitrary"),
        name="rg_lru",
    )(rest, rest, conv_w, conv_b.reshape(1, LRU_W), w_gates, b_gates, lam_p, h0)


def _dft_direct_kernel(x_ref, w_ref, fc_ref, fs_ref, o_ref, *, scale):
    y = jnp.dot(x_ref[0], w_ref[...], preferred_element_type=F32)
    yc = y[:, :FOURIER_W].astype(BF16)
    ys = y[:, FOURIER_W:].astype(BF16)
    acc = jnp.dot(fc_ref[...], yc, preferred_element_type=F32)
    acc = acc - jnp.dot(fs_ref[...], ys, preferred_element_type=F32)
    o_ref[...] = (acc * scale).astype(BF16)


def _fourier_direct(xf, chan_dft, seq_cos, seq_sin):
    B, L, _ = xf.shape
    return pl.pallas_call(
        functools.partial(_dft_direct_kernel, scale=(L * FOURIER_GW) ** -0.5),
        grid=(B,),
        in_specs=[pl.BlockSpec((1, L, FOURIER_W), lambda b: (b, 0, 0)),
                  pl.BlockSpec((FOURIER_W, 2 * FOURIER_W), lambda b: (0, 0)),
                  pl.BlockSpec((L, L), lambda b: (0, 0)),
                  pl.BlockSpec((L, L), lambda b: (0, 0))],
        out_specs=pl.BlockSpec((L, FOURIER_W), lambda b: (b, 0)),
        out_shape=jax.ShapeDtypeStruct((B * L, FOURIER_W), BF16),
        compiler_params=_cparams("arbitrary"),
        name="fourier_direct",
    )(xf, chan_dft, seq_cos, seq_sin)


FFT_RADIX = 64
FFT_COLS = 2048
FFT_K1_PER_STEP = 8


def _fft_stage1_kernel(x_ref, f_ref, tc_ref, ts_ref, o_ref):
    a = jnp.dot(f_ref[...], x_ref[0], preferred_element_type=F32)
    a_re, a_im = a[:FFT_RADIX], a[FFT_RADIX:]
    tc, ts = tc_ref[...], ts_ref[...]
    o_ref[0, 0] = (a_re * tc + a_im * ts).astype(BF16)
    o_ref[0, 1] = (a_im * tc - a_re * ts).astype(BF16)


def _fft_stage2_kernel(b_ref, f_ref, w_ref, p_ref, o_ref, *, scale):
    zs = []
    for j in range(FFT_K1_PER_STEP):
        bj = jnp.concatenate([b_ref[0, 0, j], b_ref[0, 1, j]], axis=0)
        z = jnp.dot(f_ref[...], bj, preferred_element_type=F32)
        zs.append(jnp.concatenate([z[:FFT_RADIX], z[FFT_RADIX:]], axis=1))
    z_all = jnp.concatenate(zs, axis=0).astype(BF16)
    out = (jnp.dot(z_all, w_ref[...], preferred_element_type=F32) * scale).astype(BF16)
    out = jnp.dot(p_ref[...], out, preferred_element_type=F32)
    o_ref[0] = out.reshape(FFT_RADIX, FFT_K1_PER_STEP, FOURIER_W).astype(BF16)


def _fourier_fft(xf, stage1_dft, twiddle_cos, twiddle_sin, stage2_dft, chan_dft_stacked):
    B, L, _ = xf.shape
    R = FFT_RADIX
    assert L == R * R
    cols = R * FOURIER_W
    x2 = xf.reshape(B, R, cols)
    mid = pl.pallas_call(
        _fft_stage1_kernel,
        grid=(cols // FFT_COLS, B),
        in_specs=[pl.BlockSpec((1, R, FFT_COLS), lambda j, b: (b, 0, j)),
                  pl.BlockSpec((2 * R, R), lambda j, b: (0, 0)),
                  pl.BlockSpec((R, FFT_COLS), lambda j, b: (0, j)),
                  pl.BlockSpec((R, FFT_COLS), lambda j, b: (0, j))],
        out_specs=pl.BlockSpec((1, 2, R, FFT_COLS), lambda j, b: (b, 0, 0, j)),
        out_shape=jax.ShapeDtypeStruct((B, 2, R, cols), BF16),
        compiler_params=_cparams("arbitrary", "arbitrary"),
        name="fft_stage1",
    )(x2, stage1_dft, twiddle_cos, twiddle_sin)
    mid = mid.reshape(B, 2, R, R, FOURIER_W)
    kb = FFT_K1_PER_STEP
    src = np.arange(kb * R).reshape(kb, R).T.reshape(-1)
    perm = jnp.asarray(np.eye(kb * R)[src], dtype=F32).astype(BF16)
    out = pl.pallas_call(
        functools.partial(_fft_stage2_kernel, scale=(L * FOURIER_GW) ** -0.5),
        grid=(B, R // kb),
        in_specs=[pl.BlockSpec((1, 2, kb, R, FOURIER_W), lambda b, i: (b, 0, i, 0, 0)),
                  pl.BlockSpec((2 * R, 2 * R), lambda b, i: (0, 0)),
                  pl.BlockSpec((2 * FOURIER_W, FOURIER_W), lambda b, i: (0, 0)),
                  pl.BlockSpec((kb * R, kb * R), lambda b, i: (0, 0))],
        out_specs=pl.BlockSpec((1, R, kb, FOURIER_W), lambda b, i: (b, 0, i, 0)),
        out_shape=jax.ShapeDtypeStruct((B, R, R, FOURIER_W), BF16),
        compiler_params=_cparams("arbitrary", "arbitrary"),
        name="fft_stage2",
    )(mid, stage2_dft, chan_dft_stacked, perm)
    return out.reshape(B, L, FOURIER_W)


POST_TILE = 512
POST_ROW_GROUP = 256
def _post_kernel(x_ref, attn_ref, lru_ref, four_ref, mod_ref, gmix_ref, gpre_ref, gffn_ref,
                 wo_ref, wgu_ref, wd_ref, o_ref):
    mod = mod_ref[0]
    tm = x_ref.shape[1]
    groups = [slice(r, r + POST_ROW_GROUP) for r in range(0, tm, POST_ROW_GROUP)]

    def mix_out(rows):
        o = jnp.dot(attn_ref[0, rows], wo_ref[0:ATTN_W], preferred_element_type=F32)
        o = o + jnp.dot(lru_ref[0, rows], wo_ref[ATTN_W:ATTN_W + LRU_W], preferred_element_type=F32)
        return o + jnp.dot(four_ref[0, rows], wo_ref[ATTN_W + LRU_W:MIX_W], preferred_element_type=F32)

    def residual_and_prenorm(rows, o):
        x1 = x_ref[0, rows] + mod[2:3] * (_rms(o, -1) * gmix_ref[...])
        h = (_rms(x1, -1) * gpre_ref[...]) * (1.0 + mod[4:5]) + mod[3:4]
        return x1, h.astype(BF16)

    def gate_up(hb):
        gt = jnp.dot(hb, wgu_ref[:, :FFN_HIDDEN], preferred_element_type=F32)
        up = jnp.dot(hb, wgu_ref[:, FFN_HIDDEN:], preferred_element_type=F32)
        return ((gt * jax.nn.sigmoid(gt)) * up).astype(BF16)

    mixed = [mix_out(rows) for rows in groups]
    normed = [residual_and_prenorm(rows, o) for rows, o in zip(groups, mixed)]
    acts = [gate_up(hb) for _, hb in normed]
    downs = [jnp.dot(act, wd_ref[...], preferred_element_type=F32) for act in acts]
    for rows, (x1, _), f in zip(groups, normed, downs):
        o_ref[0, rows] = x1 + mod[5:6] * (_rms(f, -1) * gffn_ref[...])


def _post(x, attn, lru, four, mod, goff, g_post_mix, g_pre_ffn, g_post_ffn,
          w_out, w_gate_up, w_down, tm):
    G, T, _ = x.shape
    vec = pl.BlockSpec((1, D_MODEL), lambda g, i: (0, 0))
    return pl.pallas_call(
        _post_kernel,
        grid=(G, T // tm),
        in_specs=[pl.BlockSpec((1, tm, D_MODEL), lambda g, i: (g, i, 0)),
                  pl.BlockSpec((1, tm, ATTN_W), lambda g, i: (g, i, 0)),
                  pl.BlockSpec((1, tm, LRU_W), lambda g, i: (g, i, 0)),
                  pl.BlockSpec((1, tm, FOURIER_W), lambda g, i: (g, i, 0)),
                  pl.BlockSpec((1, N_MOD, D_MODEL), lambda g, i: (g + goff, 0, 0)),
                  vec, vec, vec,
                  _resident((MIX_W, D_MODEL), lambda g, i: (0, 0)),
                  _resident((D_MODEL, 2 * FFN_HIDDEN), lambda g, i: (0, 0)),
                  _resident((FFN_HIDDEN, D_MODEL), lambda g, i: (0, 0))],
        out_specs=pl.BlockSpec((1, tm, D_MODEL), lambda g, i: (g, i, 0)),
        out_shape=jax.ShapeDtypeStruct((G, T, D_MODEL), F32),
        compiler_params=_cparams("arbitrary", "arbitrary"),
        name="post",
    )(x, attn, lru, four, mod, g_post_mix, g_pre_ffn, g_post_ffn, w_out, w_gate_up, w_down)


def _rope_tables(n_tokens):
    rows = n_tokens // GRID_W
    row = jnp.repeat(jnp.arange(rows, dtype=F32), GRID_W)
    col = jnp.tile(jnp.arange(GRID_W, dtype=F32), rows)
    n = HEAD_DIM // 4
    inv = ROPE_BASE ** (-jnp.arange(n, dtype=F32) / n)
    ang_r, ang_c = row[:, None] * inv, col[:, None] * inv
    zero = jnp.zeros_like(ang_r)
    cos64 = jnp.concatenate([jnp.cos(ang_r)] * 2 + [jnp.cos(ang_c)] * 2, axis=-1)
    sin_lo = jnp.concatenate([-jnp.sin(ang_r), zero, -jnp.sin(ang_c), zero], axis=-1)
    sin_hi = jnp.concatenate([zero, jnp.sin(ang_r), zero, jnp.sin(ang_c)], axis=-1)
    return tuple(jnp.tile(t, (1, 2)) for t in (cos64, sin_lo, sin_hi))


def _seq_dft_tables(n):
    r = 1 << (n.bit_length() // 2)
    l = jnp.arange(n, dtype=jnp.int32)

    def cos_sin(rows, period):
        k = jnp.arange(rows, dtype=jnp.int32)
        ang = ((k[:, None] * l[None, :]) % period).astype(F32) * (2.0 * math.pi / period)
        return jnp.cos(ang), jnp.sin(ang)

    ca, sa = cos_sin(n // r, n // r)
    cb, sb = cos_sin(r, n)
    cos = ca[:, None, :] * cb[None] - sa[:, None, :] * sb[None]
    sin = sa[:, None, :] * cb[None] + ca[:, None, :] * sb[None]
    return cos.reshape(n, n).astype(BF16), sin.reshape(n, n).astype(BF16)


def _chan_dft_table():
    k = np.arange(FOURIER_GW)
    ang = 2.0 * np.pi * ((k[:, None] * k[None, :]) % FOURIER_GW) / FOURIER_GW
    eye = np.eye(FOURIER_GROUPS)
    table = np.concatenate([np.kron(eye, np.cos(ang)), np.kron(eye, np.sin(ang))], axis=1)
    return jnp.asarray(table, dtype=F32).astype(BF16)


def _fft_tables(n):
    r = FFT_RADIX
    k = np.arange(r)
    ang = 2.0 * np.pi * ((k[:, None] * k[None, :]) % r) / r
    cm, sm = np.cos(ang), np.sin(ang)
    stage1 = np.concatenate([cm, -sm], axis=0)
    stage2 = np.block([[cm, sm], [-sm, cm]])
    kj = jnp.arange(r, dtype=jnp.int32)
    tw = (kj[:, None] * kj[None, :]).astype(F32) * (2.0 * math.pi / n)
    expand = lambda t: jnp.repeat(t, FOURIER_W, axis=1)
    return (jnp.asarray(stage1, dtype=F32).astype(BF16), expand(jnp.cos(tw)), expand(jnp.sin(tw)),
            jnp.asarray(stage2, dtype=F32).astype(BF16))


def _block_diag(w):
    n, bw, _ = w.shape
    eye = jnp.eye(n, dtype=w.dtype)
    return (eye[:, None, :, None] * w[:, :, None, :]).reshape(n * bw, n * bw)


def kernel(x_prompt, x_sample, cache_k, cache_v, state_lru, c, c_ctx, w_mod, b_mod, g_pre_mix, g_post_mix, g_pre_ffn, g_post_ffn, w_in, w_out, w_lambda, g_subln, conv_w, conv_b, lru_wa, lru_ba, lru_wx, lru_bx, lru_lambda, w_gate_up, w_down):
    n_ctx, ctx_len, _ = x_prompt.shape
    n_lat, lat_len, _ = x_sample.shape
    past = cache_k.shape[2]

    cond = jnp.concatenate(
        [c_ctx[None, :], c, jnp.zeros((COND_ROWS - 1 - n_lat, D_MODEL), F32)], axis=0)
    mod_all = _modulation(cond, w_mod, b_mod).reshape(DEPTH, COND_ROWS, N_MOD, D_MODEL)

    rope_tabs = _rope_tables(lat_len)
    chan_dft = _chan_dft_table()
    chan_dft_stacked = jnp.concatenate([chan_dft[:, :FOURIER_W], chan_dft[:, FOURIER_W:]], axis=0)
    dft_ctx = _seq_dft_tables(ctx_len)
    fft_lat = _fft_tables(lat_len)
    ck = cache_k.reshape(n_lat, DEPTH, past, ATTN_W)
    cv = cache_v.reshape(n_lat, DEPTH, past, ATTN_W)
    h0_ctx = jnp.zeros((n_ctx, 2, LRU_W), F32)

    xp = x_prompt.reshape(1, n_ctx * ctx_len, D_MODEL)
    xs = x_sample
    kv_bufs = [jnp.zeros((n_ctx, DEPTH, ctx_len, ATTN_W), F32) for _ in range(2)]
    new_h = []
    for l in range(DEPTH):
        li = _lambda_init(l)
        mod = mod_all[l]
        w_in_l = w_in[l].astype(BF16)
        w_out_l = w_out[l].astype(BF16)
        w_gu_l = w_gate_up[l].astype(BF16)
        w_down_l = w_down[l].astype(BF16)
        w_gates = jnp.concatenate(
            [_block_diag(lru_wa[l, 0]), _block_diag(lru_wx[l, 0]),
             _block_diag(lru_wa[l, 1]), _block_diag(lru_wx[l, 1])], axis=1).astype(BF16)
        b_gates = jnp.concatenate(
            [lru_ba[l, 0], lru_bx[l, 0], lru_ba[l, 1], lru_bx[l, 1]])[None, :]
        vecs = [g[l][None, :] for g in (g_post_mix, g_pre_ffn, g_post_ffn)]

        qt, k, vt, rest, xf, *kv_bufs = _inproj(
            xp, mod, 0, g_pre_mix[l][None, :], w_in_l, None,
            kv_out=(l, ctx_len, kv_bufs))
        attn = _attention(qt, k, vt, None, w_lambda[l], g_subln[l], li,
                          n_seq=n_ctx, seq_len=ctx_len, n_heads=N_HEADS,
                          n_sub=ctx_len // QUERY_BLOCK)
        rest_seq = rest.reshape(n_ctx, ctx_len, REST_W)
        lru, h_fin = _rg_lru(rest_seq, conv_w[l], conv_b[l], w_gates, b_gates,
                             lru_lambda[l], h0_ctx)
        new_h.append(h_fin)
        four = _fourier_direct(xf.reshape(n_ctx, ctx_len, FOURIER_W), chan_dft, *dft_ctx)
        xp = _post(xp, attn, lru.reshape(1, n_ctx * ctx_len, LRU_W),
                   four.reshape(1, n_ctx * ctx_len, FOURIER_W), mod, 0,
                   *vecs, w_out_l, w_gu_l, w_down_l, tm=POST_TILE)

        qt, k, vt, rest, xf = _inproj(
            xs, mod, 1, g_pre_mix[l][None, :], w_in_l, rope_tabs)
        attn = _attention(qt, k, vt, (ck, cv, l), w_lambda[l], g_subln[l], li,
                          n_seq=n_lat, seq_len=lat_len, n_heads=1, n_sub=8)
        lru, _ = _rg_lru(rest, conv_w[l], conv_b[l], w_gates, b_gates,
                         lru_lambda[l], state_lru[:, l])
        four = _fourier_fft(xf, fft_lat[0], fft_lat[1], fft_lat[2], fft_lat[3], chan_dft_stacked)
        xs = _post(xs, attn, lru, four, mod, 1,
                   *vecs, w_out_l, w_gu_l, w_down_l, tm=POST_TILE)

    return (xp.reshape(n_ctx, ctx_len, D_MODEL), xs,
            kv_bufs[0].reshape(n_ctx, DEPTH, ctx_len, N_HEADS, 2, HEAD_DIM),
            kv_bufs[1].reshape(n_ctx, DEPTH, ctx_len, N_HEADS, HEAD_W), jnp.stack(new_h, axis=1))
```

```python
import functools
import math

import jax
import jax.numpy as jnp
import numpy as np
from jax import lax
from jax.experimental import pallas as pl
from jax.experimental.pallas import tpu as pltpu

D_MODEL = 1024
DEPTH = 2
GRID_W = 64
HEAD_DIM = 64
N_HEADS = 4
HEAD_W = 2 * HEAD_DIM
ATTN_W = N_HEADS * HEAD_W
LRU_W = D_MODEL // 4
LRU_BLOCKS = 4
CONV_W = 4
FOURIER_W = D_MODEL // 4
FOURIER_GROUPS = 4
FOURIER_GW = FOURIER_W // FOURIER_GROUPS
MIX_W = ATTN_W + LRU_W + FOURIER_W
REST_W = 2 * LRU_W
IN_W = 3 * ATTN_W + REST_W + FOURIER_W
FFN_HIDDEN = -(-8 * D_MODEL // (3 * 256)) * 256
ROPE_BASE = 10000.0
RG_C = 8.0
EPS = 1e-6
N_MOD = 6
COND_ROWS = 8

F32 = jnp.float32
BF16 = jnp.bfloat16
V7X_VMEM_LIMIT_BYTES = 56 * 1024 * 1024
CONV_HALO = 8


def _cparams(*semantics, flags=None):
    return pltpu.CompilerParams(dimension_semantics=semantics,
                                vmem_limit_bytes=V7X_VMEM_LIMIT_BYTES, flags=flags)


def _resident(shape, index_map):
    return pl.BlockSpec(shape, index_map, pipeline_mode=pl.Buffered(1))


def _lambda_init(l):
    return 0.8 - 0.6 * math.exp(-0.3 * l)


def _rms(x, axis):
    return x * lax.rsqrt(jnp.mean(x * x, axis=axis, keepdims=True) + EPS)


def _mod_kernel(cond_ref, w_ref, b_ref, o_ref):
    s = cond_ref[...]
    s = s * jax.nn.sigmoid(s)
    o_ref[0] = jnp.dot(s, w_ref[0], precision=lax.Precision.HIGHEST,
                       preferred_element_type=F32) + b_ref[0]


def _modulation(cond, w_mod, b_mod):
    tn = 1536
    n_out = N_MOD * D_MODEL
    return pl.pallas_call(
        _mod_kernel,
        grid=(DEPTH, n_out // tn),
        in_specs=[pl.BlockSpec((COND_ROWS, D_MODEL), lambda l, j: (0, 0)),
                  pl.BlockSpec((1, D_MODEL, tn), lambda l, j: (l, 0, j)),
                  pl.BlockSpec((1, 1, tn), lambda l, j: (l, 0, j))],
        out_specs=pl.BlockSpec((1, COND_ROWS, tn), lambda l, j: (l, 0, j)),
        out_shape=jax.ShapeDtypeStruct((DEPTH, COND_ROWS, n_out), F32),
        compiler_params=_cparams("arbitrary", "arbitrary"),
        name="modulation",
    )(cond, w_mod, b_mod.reshape(DEPTH, 1, n_out))


INPROJ_TILE = 512
INPROJ_ROW_GROUP = 256


def _inproj_kernel(*refs, rope, kv_mode, n_prev):
    x_ref, mod_ref, g_ref, w_ref = refs[:4]
    refs = refs[4:]
    if rope:
        cos_ref, sin_lo_ref, sin_hi_ref = refs[:3]
        refs = refs[3:]
    prev_k, prev_v = refs[:n_prev], refs[n_prev:2 * n_prev]
    refs = refs[2 * n_prev:]
    qt_ref, k_ref, vt_ref, rest_ref, xf_ref = refs[:5]
    mod = mod_ref[0]
    tm = x_ref.shape[1]
    groups = [slice(r, r + INPROJ_ROW_GROUP) for r in range(0, tm, INPROJ_ROW_GROUP)]
    hs = [((_rms(x_ref[0, rows], -1) * g_ref[...]) * (1.0 + mod[1:2]) + mod[0:1]).astype(BF16)
          for rows in groups]
    projs = [jnp.dot(h, w_ref[...], preferred_element_type=F32) for h in hs]

    row = lax.broadcasted_iota(jnp.int32, (HEAD_W, INPROJ_ROW_GROUP), 0)
    scale = HEAD_DIM ** -0.5 * math.log2(math.e)
    for gi, (rows, proj) in enumerate(zip(groups, projs)):
        q = proj[:, :ATTN_W]
        k = proj[:, ATTN_W:2 * ATTN_W]
        v = proj[:, 2 * ATTN_W:3 * ATTN_W]
        if kv_mode == "flat":
            kf_ref, vf_ref = refs[5:7]
            kf_ref[rows] = k
            vf_ref[rows] = v
        elif kv_mode == "stacked":
            kf_ref, vf_ref = refs[5:7]
            for p in range(n_prev):
                kf_ref[gi, p] = prev_k[p][rows]
                vf_ref[gi, p] = prev_v[p][rows]
            kf_ref[gi, n_prev] = k
            vf_ref[gi, n_prev] = v

        def rotary(t):
            return (t * cos_ref[rows] + pltpu.roll(t, HEAD_W - 16, 1) * sin_lo_ref[rows]
                    + pltpu.roll(t, 16, 1) * sin_hi_ref[rows])

        for hd in range(N_HEADS):
            qh = q[:, hd * HEAD_W:(hd + 1) * HEAD_W]
            kh = k[:, hd * HEAD_W:(hd + 1) * HEAD_W]
            if rope:
                qh = rotary(qh)
                kh = rotary(kh)
            qht = (qh * scale).T
            qt_ref[0, 2 * hd, :, rows] = jnp.where(row < HEAD_DIM, qht, 0.0).astype(BF16)
            qt_ref[0, 2 * hd + 1, :, rows] = jnp.where(row >= HEAD_DIM, qht, 0.0).astype(BF16)
            k_ref[0, rows, hd * HEAD_W:(hd + 1) * HEAD_W] = kh.astype(BF16)
        vt_ref[0, :, rows] = v.T.astype(BF16)
        rest_ref[0, rows] = proj[:, 3 * ATTN_W:3 * ATTN_W + REST_W]
        xf_ref[0, rows] = proj[:, 3 * ATTN_W + REST_W:].astype(BF16)


def _inproj(x, mod, goff, g_pre, w_in, rope_tabs, kv_out=None):
    G, T, _ = x.shape
    tm = INPROJ_TILE
    rope = rope_tabs is not None
    in_specs = [pl.BlockSpec((1, tm, D_MODEL), lambda g, i: (g, i, 0)),
                pl.BlockSpec((1, N_MOD, D_MODEL), lambda g, i: (g + goff, 0, 0)),
                pl.BlockSpec((1, D_MODEL), lambda g, i: (0, 0)),
                _resident((D_MODEL, IN_W), lambda g, i: (0, 0))]
    args = [x, mod, g_pre, w_in]
    if rope:
        in_specs += [pl.BlockSpec((tm, HEAD_W), lambda g, i: (i, 0))] * 3
        args += list(rope_tabs)
    out_specs = [pl.BlockSpec((1, 2 * N_HEADS, HEAD_W, tm), lambda g, i: (g, 0, 0, i)),
                 pl.BlockSpec((1, tm, ATTN_W), lambda g, i: (g, i, 0)),
                 pl.BlockSpec((1, ATTN_W, tm), lambda g, i: (g, 0, i)),
                 pl.BlockSpec((1, tm, REST_W), lambda g, i: (g, i, 0)),
                 pl.BlockSpec((1, tm, FOURIER_W), lambda g, i: (g, i, 0))]
    out_shape = [jax.ShapeDtypeStruct((G, 2 * N_HEADS, HEAD_W, T), BF16),
                 jax.ShapeDtypeStruct((G, T, ATTN_W), BF16),
                 jax.ShapeDtypeStruct((G, ATTN_W, T), BF16),
                 jax.ShapeDtypeStruct((G, T, REST_W), F32),
                 jax.ShapeDtypeStruct((G, T, FOURIER_W), BF16)]
    kv_mode = None
    n_prev = 0
    if kv_out is not None:
        assert G == 1
        seq_len, last, prev_k, prev_v = kv_out
        if not last:
            kv_mode = "flat"
            out_specs += [pl.BlockSpec((tm, ATTN_W), lambda g, i: (i, 0))] * 2
            out_shape += [jax.ShapeDtypeStruct((T, ATTN_W), F32)] * 2
        else:
            assert seq_len == INPROJ_ROW_GROUP
            kv_mode = "stacked"
            n_prev = len(prev_k)
            in_specs += [pl.BlockSpec((tm, ATTN_W), lambda g, i: (i, 0))] * (2 * n_prev)
            args += list(prev_k) + list(prev_v)
            out_specs += [pl.BlockSpec((tm // seq_len, n_prev + 1, seq_len, ATTN_W),
                                       lambda g, i: (i, 0, 0, 0))] * 2
            out_shape += [jax.ShapeDtypeStruct((T // seq_len, n_prev + 1, seq_len, ATTN_W), F32)] * 2
    return pl.pallas_call(
        functools.partial(_inproj_kernel, rope=rope, kv_mode=kv_mode, n_prev=n_prev),
        grid=(G, T // tm),
        in_specs=in_specs, out_specs=out_specs, out_shape=out_shape,
        compiler_params=_cparams("arbitrary", "arbitrary"),
        name="inproj_rope" if rope else "inproj_ctx",
    )(*args)


QUERY_BLOCK = 256
KEY_CHUNK = 256
SUBLANES = 8
SUM_ACCUMULATORS = 2


ROW_ACCUMULATORS = 2


def _accumulate_rows(accs, x, op):
    for idx, r in enumerate(range(0, x.shape[0], SUBLANES)):
        j = idx % len(accs)
        part = x[r:r + SUBLANES]
        accs[j] = part if accs[j] is None else op(accs[j], part)


def _finish_rows(accs, op, reduce):
    total = accs[0]
    for a in accs[1:]:
        total = op(total, a)
    return reduce(total, axis=0, keepdims=True)


def _attn_kernel(*refs, has_cache, li, n_heads, n_sub, n_keys):
    qt_ref, k_ref, vt_ref = refs[:3]
    refs = refs[3:]
    if has_cache:
        ck_ref, cv_ref = refs[:2]
        refs = refs[2:]
    wl_ref, g_ref, o_ref = refs[:3]
    s_bufs = refs[3:5]
    acc_bufs = refs[5:7]
    n_self = k_ref.shape[1]
    if has_cache:
        kall, vtall = refs[7:9]

        @pl.when(pl.program_id(2) == 0)
        def _():
            kall[0:n_self] = k_ref[0]
            kall[n_self:n_keys] = ck_ref[0].astype(BF16)
            vtall[:, 0:n_self] = vt_ref[0]
            vtall[:, n_self:n_keys] = cv_ref[0].T.astype(BF16)

    def k_chunk(hd, c):
        rows = slice(c * KEY_CHUNK, (c + 1) * KEY_CHUNK)
        return kall[rows, :] if has_cache else k_ref[0, rows, hd * HEAD_W:(hd + 1) * HEAD_W]

    def vt_chunk(hd, c):
        cols = slice(c * KEY_CHUNK, (c + 1) * KEY_CHUNK)
        return vtall[:, cols] if has_cache else vt_ref[0, hd * HEAD_W:(hd + 1) * HEAD_W, cols]

    units = [(hd, sub, m) for hd in range(n_heads) for sub in range(n_sub) for m in range(2)]
    n_chunks = n_keys // KEY_CHUNK
    results = {}
    col_max = None
    for t in range(len(units) + 1):
        if t < len(units):
            hd, sub, m = units[t]
            qt = qt_ref[0, 2 * hd + m, :, sub * QUERY_BLOCK:(sub + 1) * QUERY_BLOCK]
            new_max = [None] * ROW_ACCUMULATORS
        if t > 0:
            hd_p = units[t - 1][0]
            mx = _finish_rows(col_max, jnp.maximum, jnp.max)
            den = [None] * SUM_ACCUMULATORS
            acc_ref = acc_bufs[(t - 1) % 2]
        for c in range(n_chunks):
            rows = slice(c * KEY_CHUNK, (c + 1) * KEY_CHUNK)
            if t < len(units):
                s = jnp.dot(k_chunk(hd, c), qt, preferred_element_type=F32)
                s_bufs[t % 2][rows, :] = s
                _accumulate_rows(new_max, s, jnp.maximum)
            if t > 0:
                e = jnp.exp2(s_bufs[(t - 1) % 2][rows, :] - mx)
                _accumulate_rows(den, e, jnp.add)
                pv = jnp.dot(vt_chunk(hd_p, c), e.astype(BF16), preferred_element_type=F32)
                if c == 0:
                    acc_ref[...] = pv
                else:
                    acc_ref[...] += pv
        if t > 0:
            results[units[t - 1]] = acc_ref[...] / _finish_rows(den, jnp.add, jnp.sum)
        col_max = new_max

    wl = wl_ref[...]
    lam = (jnp.exp(jnp.sum(wl[0:1] * wl[1:2], axis=-1, keepdims=True))
           - jnp.exp(jnp.sum(wl[2:3] * wl[3:4], axis=-1, keepdims=True)) + li)
    for hd in range(n_heads):
        for sub in range(n_sub):
            ot = results[(hd, sub, 0)] - lam * results[(hd, sub, 1)]
            y = (_rms(ot, 0) * g_ref[...]) * (1.0 - li)
            o_ref[0, sub * QUERY_BLOCK:(sub + 1) * QUERY_BLOCK,
                  hd * HEAD_W:(hd + 1) * HEAD_W] = y.T.astype(BF16)


def _attention(qt, k, vt, cache, w_lambda_l, g_subln_l, li, n_seq, seq_len, n_heads, n_sub):
    G, _, _, T = qt.shape
    per_g = T // seq_len
    tq = n_sub * QUERY_BLOCK
    nq = seq_len // tq
    has_cache = cache is not None
    n_keys = seq_len

    def gi(b):
        return b // per_g

    def si(b):
        return b % per_g

    hw = n_heads * HEAD_W
    in_specs = [pl.BlockSpec((1, 2 * n_heads, HEAD_W, tq), lambda b, h, i: (gi(b), h, 0, si(b) * nq + i)),
                pl.BlockSpec((1, seq_len, hw), lambda b, h, i: (gi(b), si(b), h)),
                pl.BlockSpec((1, hw, seq_len), lambda b, h, i: (gi(b), h, si(b)))]
    args = [qt, k, vt]
    if has_cache:
        assert n_heads == 1
        ck, cv, layer = cache
        past = ck.shape[2]
        n_keys = seq_len + past
        in_specs += [pl.BlockSpec((1, None, past, HEAD_W), lambda b, h, i: (b, layer, 0, h))] * 2
        args += [ck, cv]
    scratch = [pltpu.VMEM((n_keys, QUERY_BLOCK), F32)] * 2
    scratch += [pltpu.VMEM((HEAD_W, QUERY_BLOCK), F32)] * 2
    if has_cache:
        scratch += [pltpu.VMEM((n_keys, HEAD_W), BF16), pltpu.VMEM((HEAD_W, n_keys), BF16)]
    in_specs += [pl.BlockSpec((4, HEAD_DIM), lambda b, h, i: (0, 0)),
                 pl.BlockSpec((HEAD_W, 1), lambda b, h, i: (0, 0))]
    args += [w_lambda_l, g_subln_l.reshape(HEAD_W, 1)]
    return pl.pallas_call(
        functools.partial(_attn_kernel, has_cache=has_cache, li=li, n_heads=n_heads,
                          n_sub=n_sub, n_keys=n_keys),
        grid=(n_seq, N_HEADS // n_heads, nq),
        in_specs=in_specs,
        out_specs=pl.BlockSpec((1, tq, hw), lambda b, h, i: (gi(b), si(b) * nq + i, h)),
        out_shape=jax.ShapeDtypeStruct((G, T, ATTN_W), BF16),
        scratch_shapes=scratch,
        compiler_params=_cparams("arbitrary", "arbitrary", "arbitrary"),
        name="attn_latent" if has_cache else "attn_ctx",
    )(*args)


def _scan_chunk(a, b, carry, reverse):
    n, w = a.shape
    groups = n // SUBLANES
    a = a.reshape(groups, SUBLANES, w)
    b = b.reshape(groups, SUBLANES, w)
    sub = lax.broadcasted_iota(jnp.int32, (groups, SUBLANES, w), 1)
    d = 1
    while d < SUBLANES:
        if reverse:
            shift, valid = SUBLANES - d, sub < SUBLANES - d
        else:
            shift, valid = d, sub >= d
        a_prev = jnp.where(valid, pltpu.roll(a, shift, 1), 1.0)
        b_prev = jnp.where(valid, pltpu.roll(b, shift, 1), 0.0)
        b = a * b_prev + b
        a = a * a_prev
        d *= 2
    edge = 0 if reverse else SUBLANES - 1
    hs = [None] * groups
    for g in (range(groups - 1, -1, -1) if reverse else range(groups)):
        hs[g] = a[g] * carry + b[g]
        carry = hs[g][edge:edge + 1]
    return jnp.concatenate(hs, axis=0), carry


def _lru_kernel(xr_ref, gr_ref, cw_ref, cb_ref, w_ref, bias_ref, lam_ref, h0_ref,
                out_ref, hfin_ref, xpad, hf_scr, ab_scr, bb_scr, *, seq_len, tc):
    nch = seq_len // tc
    zeros_halo = jnp.zeros((CONV_HALO, LRU_W), F32)
    xpad[0:CONV_HALO] = zeros_halo
    xpad[seq_len + CONV_HALO:seq_len + 2 * CONV_HALO] = zeros_halo
    xpad[CONV_HALO:seq_len + CONV_HALO] = xr_ref[0]
    lam_p = lam_ref[...]
    neg = -lam_p
    softplus = jnp.maximum(neg, 0.0) + jnp.log1p(jnp.exp(-jnp.abs(neg)))
    coef = -RG_C * softplus
    cw = cw_ref[...]
    h0 = h0_ref[0]

    def decay_and_input(xc, z, direction):
        off = direction * 2 * LRU_W
        r = jax.nn.sigmoid(z[:, off:off + LRU_W])
        i = jax.nn.sigmoid(z[:, off + LRU_W:off + 2 * LRU_W])
        log_a = coef[direction:direction + 1] * r
        a = jnp.exp(log_a)
        b = jnp.sqrt(-jnp.tanh(log_a) * (1.0 + a * a)) * (i * xc)
        return a, b

    def forward_chunk(c, carry):
        start = pl.multiple_of(c * tc, tc)
        win = xpad[pl.ds(start, tc + 2 * CONV_HALO), :]
        lo = CONV_HALO - CONV_W // 2
        xc = win[lo:lo + tc] * cw[0:1]
        for j in range(1, CONV_W):
            xc = xc + win[lo + j:lo + j + tc] * cw[j:j + 1]
        xc = xc + cb_ref[...]
        z = jnp.dot(xc.astype(BF16), w_ref[...], preferred_element_type=F32) + bias_ref[...]
        a_f, b_f = decay_and_input(xc, z, 0)
        hf, carry = _scan_chunk(a_f, b_f, carry, reverse=False)
        hf_scr[pl.ds(start, tc), :] = hf
        a_b, b_b = decay_and_input(xc, z, 1)
        ab_scr[pl.ds(start, tc), :] = a_b
        bb_scr[pl.ds(start, tc), :] = b_b
        return carry

    def backward_chunk(j, carry):
        c = nch - 1 - j
        start = pl.multiple_of(c * tc, tc)
        hb, carry = _scan_chunk(ab_scr[pl.ds(start, tc), :], bb_scr[pl.ds(start, tc), :],
                                carry, reverse=True)
        gate = jax.nn.gelu(gr_ref[0, pl.ds(start, tc), :], approximate=True)
        out_ref[0, pl.ds(start, tc), :] = ((hf_scr[pl.ds(start, tc), :] + hb) * gate).astype(BF16)
        return carry

    if nch == 1:
        hf_fin = forward_chunk(0, h0[0:1])
        hb_fin = backward_chunk(0, h0[1:2])
    else:
        hf_fin = lax.fori_loop(0, nch, forward_chunk, h0[0:1])
        hb_fin = lax.fori_loop(0, nch, backward_chunk, h0[1:2])
    hfin_ref[0, 0:1] = hf_fin
    hfin_ref[0, 1:2] = hb_fin


def _rg_lru(rest, conv_w, conv_b, w_gates, b_gates, lam_p, h0):
    B, L, _ = rest.shape
    tc = min(L, 256)
    return pl.pallas_call(
        functools.partial(_lru_kernel, seq_len=L, tc=tc),
        grid=(B,),
        in_specs=[pl.BlockSpec((1, L, LRU_W), lambda b: (b, 0, 0)),
                  pl.BlockSpec((1, L, LRU_W), lambda b: (b, 0, 1)),
                  pl.BlockSpec((CONV_W, LRU_W), lambda b: (0, 0)),
                  pl.BlockSpec((1, LRU_W), lambda b: (0, 0)),
                  pl.BlockSpec((LRU_W, 4 * LRU_W), lambda b: (0, 0)),
                  pl.BlockSpec((1, 4 * LRU_W), lambda b: (0, 0)),
                  pl.BlockSpec((2, LRU_W), lambda b: (0, 0)),
                  pl.BlockSpec((1, 2, LRU_W), lambda b: (b, 0, 0))],
        out_specs=[pl.BlockSpec((1, L, LRU_W), lambda b: (b, 0, 0)),
                   pl.BlockSpec((1, 2, LRU_W), lambda b: (b, 0, 0))],
        out_shape=[jax.ShapeDtypeStruct((B, L, LRU_W), BF16),
                   jax.ShapeDtypeStruct((B, 2, LRU_W), F32)],
        scratch_shapes=[pltpu.VMEM((L + 2 * CONV_HALO, LRU_W), F32),
                        pltpu.VMEM((L, LRU_W), F32),
                        pltpu.VMEM((L, LRU_W), F32),
                        pltpu.VMEM((L, LRU_W), F32)],
        compiler_params=_cparams("arbitrary"),
        name="rg_lru",
    )(rest, rest, conv_w, conv_b.reshape(1, LRU_W), w_gates, b_gates, lam_p, h0)


def _dft_direct_kernel(x_ref, w_ref, fc_ref, fs_ref, o_ref, *, scale):
    y = jnp.dot(x_ref[0], w_ref[...], preferred_element_type=F32)
    yc = y[:, :FOURIER_W].astype(BF16)
    ys = y[:, FOURIER_W:].astype(BF16)
    acc = jnp.dot(fc_ref[...], yc, preferred_element_type=F32)
    acc = acc - jnp.dot(fs_ref[...], ys, preferred_element_type=F32)
    o_ref[...] = (acc * scale).astype(BF16)


def _fourier_direct(xf, chan_dft, seq_cos, seq_sin):
    B, L, _ = xf.shape
    return pl.pallas_call(
        functools.partial(_dft_direct_kernel, scale=(L * FOURIER_GW) ** -0.5),
        grid=(B,),
        in_specs=[pl.BlockSpec((1, L, FOURIER_W), lambda b: (b, 0, 0)),
                  pl.BlockSpec((FOURIER_W, 2 * FOURIER_W), lambda b: (0, 0)),
                  pl.BlockSpec((L, L), lambda b: (0, 0)),
                  pl.BlockSpec((L, L), lambda b: (0, 0))],
        out_specs=pl.BlockSpec((L, FOURIER_W), lambda b: (b, 0)),
        out_shape=jax.ShapeDtypeStruct((B * L, FOURIER_W), BF16),
        compiler_params=_cparams("arbitrary"),
        name="fourier_direct",
    )(xf, chan_dft, seq_cos, seq_sin)


FFT_RADIX = 64
FFT_COLS = 2048
FFT_K1_PER_STEP = 8


def _fft_stage1_kernel(x_ref, f_ref, tc_ref, ts_ref, o_ref):
    a = jnp.dot(f_ref[...], x_ref[0], preferred_element_type=F32)
    a_re, a_im = a[:FFT_RADIX], a[FFT_RADIX:]
    tc, ts = tc_ref[...], ts_ref[...]
    o_ref[0, 0] = (a_re * tc + a_im * ts).astype(BF16)
    o_ref[0, 1] = (a_im * tc - a_re * ts).astype(BF16)


def _fft_stage2_kernel(b_ref, f_ref, w_ref, p_ref, o_ref, *, scale):
    zs = []
    for j in range(FFT_K1_PER_STEP):
        bj = jnp.concatenate([b_ref[0, 0, j], b_ref[0, 1, j]], axis=0)
        z = jnp.dot(f_ref[...], bj, preferred_element_type=F32)
        zs.append(jnp.concatenate([z[:FFT_RADIX], z[FFT_RADIX:]], axis=1))
    z_all = jnp.concatenate(zs, axis=0).astype(BF16)
    out = (jnp.dot(z_all, w_ref[...], preferred_element_type=F32) * scale).astype(BF16)
    out = jnp.dot(p_ref[...], out, preferred_element_type=F32)
    o_ref[0] = out.reshape(FFT_RADIX, FFT_K1_PER_STEP, FOURIER_W).astype(BF16)


def _fourier_fft(xf, stage1_dft, twiddle_cos, twiddle_sin, stage2_dft, chan_dft_stacked):
    B, L, _ = xf.shape
    R = FFT_RADIX
    assert L == R * R
    cols = R * FOURIER_W
    x2 = xf.reshape(B, R, cols)
    mid = pl.pallas_call(
        _fft_stage1_kernel,
        grid=(cols // FFT_COLS, B),
        in_specs=[pl.BlockSpec((1, R, FFT_COLS), lambda j, b: (b, 0, j)),
                  pl.BlockSpec((2 * R, R), lambda j, b: (0, 0)),
                  pl.BlockSpec((R, FFT_COLS), lambda j, b: (0, j)),
                  pl.BlockSpec((R, FFT_COLS), lambda j, b: (0, j))],
        out_specs=pl.BlockSpec((1, 2, R, FFT_COLS), lambda j, b: (b, 0, 0, j)),
        out_shape=jax.ShapeDtypeStruct((B, 2, R, cols), BF16),
        compiler_params=_cparams("arbitrary", "arbitrary"),
        name="fft_stage1",
    )(x2, stage1_dft, twiddle_cos, twiddle_sin)
    mid = mid.reshape(B, 2, R, R, FOURIER_W)
    kb = FFT_K1_PER_STEP
    src = np.arange(kb * R).reshape(kb, R).T.reshape(-1)
    perm = jnp.asarray(np.eye(kb * R)[src], dtype=F32).astype(BF16)
    out = pl.pallas_call(
        functools.partial(_fft_stage2_kernel, scale=(L * FOURIER_GW) ** -0.5),
        grid=(B, R // kb),
        in_specs=[pl.BlockSpec((1, 2, kb, R, FOURIER_W), lambda b, i: (b, 0, i, 0, 0)),
                  pl.BlockSpec((2 * R, 2 * R), lambda b, i: (0, 0)),
                  pl.BlockSpec((2 * FOURIER_W, FOURIER_W), lambda b, i: (0, 0)),
                  pl.BlockSpec((kb * R, kb * R), lambda b, i: (0, 0))],
        out_specs=pl.BlockSpec((1, R, kb, FOURIER_W), lambda b, i: (b, 0, i, 0)),
        out_shape=jax.ShapeDtypeStruct((B, R, R, FOURIER_W), BF16),
        compiler_params=_cparams("arbitrary", "arbitrary"),
        name="fft_stage2",
    )(mid, stage2_dft, chan_dft_stacked, perm)
    return out.reshape(B, L, FOURIER_W)


POST_TILE = 512
POST_ROW_GROUP = 256
def _post_kernel(x_ref, attn_ref, lru_ref, four_ref, mod_ref, gmix_ref, gpre_ref, gffn_ref,
                 wo_ref, wgu_ref, wd_ref, o_ref):
    mod = mod_ref[0]
    tm = x_ref.shape[1]
    groups = [slice(r, r + POST_ROW_GROUP) for r in range(0, tm, POST_ROW_GROUP)]

    def mix_out(rows):
        o = jnp.dot(attn_ref[0, rows], wo_ref[0:ATTN_W], preferred_element_type=F32)
        o = o + jnp.dot(lru_ref[0, rows], wo_ref[ATTN_W:ATTN_W + LRU_W], preferred_element_type=F32)
        return o + jnp.dot(four_ref[0, rows], wo_ref[ATTN_W + LRU_W:MIX_W], preferred_element_type=F32)

    def residual_and_prenorm(rows, o):
        x1 = x_ref[0, rows] + mod[2:3] * (_rms(o, -1) * gmix_ref[...])
        h = (_rms(x1, -1) * gpre_ref[...]) * (1.0 + mod[4:5]) + mod[3:4]
        return x1, h.astype(BF16)

    def gate_up(hb):
        gt = jnp.dot(hb, wgu_ref[:, :FFN_HIDDEN], preferred_element_type=F32)
        up = jnp.dot(hb, wgu_ref[:, FFN_HIDDEN:], preferred_element_type=F32)
        return ((gt * jax.nn.sigmoid(gt)) * up).astype(BF16)

    mixed = [mix_out(rows) for rows in groups]
    normed = [residual_and_prenorm(rows, o) for rows, o in zip(groups, mixed)]
    acts = [gate_up(hb) for _, hb in normed]
    downs = [jnp.dot(act, wd_ref[...], preferred_element_type=F32) for act in acts]
    for rows, (x1, _), f in zip(groups, normed, downs):
        o_ref[0, rows] = x1 + mod[5:6] * (_rms(f, -1) * gffn_ref[...])


def _post(x, attn, lru, four, mod, goff, g_post_mix, g_pre_ffn, g_post_ffn,
          w_out, w_gate_up, w_down, tm):
    G, T, _ = x.shape
    vec = pl.BlockSpec((1, D_MODEL), lambda g, i: (0, 0))
    return pl.pallas_call(
        _post_kernel,
        grid=(G, T // tm),
        in_specs=[pl.BlockSpec((1, tm, D_MODEL), lambda g, i: (g, i, 0)),
                  pl.BlockSpec((1, tm, ATTN_W), lambda g, i: (g, i, 0)),
                  pl.BlockSpec((1, tm, LRU_W), lambda g, i: (g, i, 0)),
                  pl.BlockSpec((1, tm, FOURIER_W), lambda g, i: (g, i, 0)),
                  pl.BlockSpec((1, N_MOD, D_MODEL), lambda g, i: (g + goff, 0, 0)),
                  vec, vec, vec,
                  _resident((MIX_W, D_MODEL), lambda g, i: (0, 0)),
                  _resident((D_MODEL, 2 * FFN_HIDDEN), lambda g, i: (0, 0)),
                  _resident((FFN_HIDDEN, D_MODEL), lambda g, i: (0, 0))],
        out_specs=pl.BlockSpec((1, tm, D_MODEL), lambda g, i: (g, i, 0)),
        out_shape=jax.ShapeDtypeStruct((G, T, D_MODEL), F32),
        compiler_params=_cparams("arbitrary", "arbitrary"),
        name="post",
    )(x, attn, lru, four, mod, g_post_mix, g_pre_ffn, g_post_ffn, w_out, w_gate_up, w_down)


def _rope_tables(n_tokens):
    rows = n_tokens // GRID_W
    row = jnp.repeat(jnp.arange(rows, dtype=F32), GRID_W)
    col = jnp.tile(jnp.arange(GRID_W, dtype=F32), rows)
    n = HEAD_DIM // 4
    inv = ROPE_BASE ** (-jnp.arange(n, dtype=F32) / n)
    ang_r, ang_c = row[:, None] * inv, col[:, None] * inv
    zero = jnp.zeros_like(ang_r)
    cos64 = jnp.concatenate([jnp.cos(ang_r)] * 2 + [jnp.cos(ang_c)] * 2, axis=-1)
    sin_lo = jnp.concatenate([-jnp.sin(ang_r), zero, -jnp.sin(ang_c), zero], axis=-1)
    sin_hi = jnp.concatenate([zero, jnp.sin(ang_r), zero, jnp.sin(ang_c)], axis=-1)
    return tuple(jnp.tile(t, (1, 2)) for t in (cos64, sin_lo, sin_hi))


def _seq_dft_tables(n):
    r = 1 << (n.bit_length() // 2)
    l = jnp.arange(n, dtype=jnp.int32)

    def cos_sin(rows, period):
        k = jnp.arange(rows, dtype=jnp.int32)
        ang = ((k[:, None] * l[None, :]) % period).astype(F32) * (2.0 * math.pi / period)
        return jnp.cos(ang), jnp.sin(ang)

    ca, sa = cos_sin(n // r, n // r)
    cb, sb = cos_sin(r, n)
    cos = ca[:, None, :] * cb[None] - sa[:, None, :] * sb[None]
    sin = sa[:, None, :] * cb[None] + ca[:, None, :] * sb[None]
    return cos.reshape(n, n).astype(BF16), sin.reshape(n, n).astype(BF16)


def _chan_dft_table():
    k = np.arange(FOURIER_GW)
    ang = 2.0 * np.pi * ((k[:, None] * k[None, :]) % FOURIER_GW) / FOURIER_GW
    eye = np.eye(FOURIER_GROUPS)
    table = np.concatenate([np.kron(eye, np.cos(ang)), np.kron(eye, np.sin(ang))], axis=1)
    return jnp.asarray(table, dtype=F32).astype(BF16)


def _fft_tables(n):
    r = FFT_RADIX
    k = np.arange(r)
    ang = 2.0 * np.pi * ((k[:, None] * k[None, :]) % r) / r
    cm, sm = np.cos(ang), np.sin(ang)
    stage1 = np.concatenate([cm, -sm], axis=0)
    stage2 = np.block([[cm, sm], [-sm, cm]])
    kj = jnp.arange(r, dtype=jnp.int32)
    tw = (kj[:, None] * kj[None, :]).astype(F32) * (2.0 * math.pi / n)
    expand = lambda t: jnp.repeat(t, FOURIER_W, axis=1)
    return (jnp.asarray(stage1, dtype=F32).astype(BF16), expand(jnp.cos(tw)), expand(jnp.sin(tw)),
            jnp.asarray(stage2, dtype=F32).astype(BF16))


def _block_diag(w):
    n, bw, _ = w.shape
    eye = jnp.eye(n, dtype=w.dtype)
    return (eye[:, None, :, None] * w[:, :, None, :]).reshape(n * bw, n * bw)


def kernel(x_prompt, x_sample, cache_k, cache_v, state_lru, c, c_ctx, w_mod, b_mod, g_pre_mix, g_post_mix, g_pre_ffn, g_post_ffn, w_in, w_out, w_lambda, g_subln, conv_w, conv_b, lru_wa, lru_ba, lru_wx, lru_bx, lru_lambda, w_gate_up, w_down):
    n_ctx, ctx_len, _ = x_prompt.shape
    n_lat, lat_len, _ = x_sample.shape
    past = cache_k.shape[2]

    cond = jnp.concatenate(
        [c_ctx[None, :], c, jnp.zeros((COND_ROWS - 1 - n_lat, D_MODEL), F32)], axis=0)
    mod_all = _modulation(cond, w_mod, b_mod).reshape(DEPTH, COND_ROWS, N_MOD, D_MODEL)

    rope_tabs = _rope_tables(lat_len)
    chan_dft = _chan_dft_table()
    chan_dft_stacked = jnp.concatenate([chan_dft[:, :FOURIER_W], chan_dft[:, FOURIER_W:]], axis=0)
    dft_ctx = _seq_dft_tables(ctx_len)
    fft_lat = _fft_tables(lat_len)
    ck = cache_k.reshape(n_lat, DEPTH, past, ATTN_W)
    cv = cache_v.reshape(n_lat, DEPTH, past, ATTN_W)
    h0_ctx = jnp.zeros((n_ctx, 2, LRU_W), F32)

    xp = x_prompt.reshape(1, n_ctx * ctx_len, D_MODEL)
    xs = x_sample
    prev_k, prev_v, new_h = [], [], []
    for l in range(DEPTH):
        li = _lambda_init(l)
        mod = mod_all[l]
        w_in_l = w_in[l].astype(BF16)
        w_out_l = w_out[l].astype(BF16)
        w_gu_l = w_gate_up[l].astype(BF16)
        w_down_l = w_down[l].astype(BF16)
        w_gates = jnp.concatenate(
            [_block_diag(lru_wa[l, 0]), _block_diag(lru_wx[l, 0]),
             _block_diag(lru_wa[l, 1]), _block_diag(lru_wx[l, 1])], axis=1).astype(BF16)
        b_gates = jnp.concatenate(
            [lru_ba[l, 0], lru_bx[l, 0], lru_ba[l, 1], lru_bx[l, 1]])[None, :]
        vecs = [g[l][None, :] for g in (g_post_mix, g_pre_ffn, g_post_ffn)]

        qt, k, vt, rest, xf, k_f32, v_f32 = _inproj(
            xp, mod, 0, g_pre_mix[l][None, :], w_in_l, None,
            kv_out=(ctx_len, l == DEPTH - 1, prev_k, prev_v))
        prev_k.append(k_f32)
        prev_v.append(v_f32)
        attn = _attention(qt, k, vt, None, w_lambda[l], g_subln[l], li,
                          n_seq=n_ctx, seq_len=ctx_len, n_heads=N_HEADS,
                          n_sub=ctx_len // QUERY_BLOCK)
        rest_seq = rest.reshape(n_ctx, ctx_len, REST_W)
        lru, h_fin = _rg_lru(rest_seq, conv_w[l], conv_b[l], w_gates, b_gates,
                             lru_lambda[l], h0_ctx)
        new_h.append(h_fin)
        four = _fourier_direct(xf.reshape(n_ctx, ctx_len, FOURIER_W), chan_dft, *dft_ctx)
        xp = _post(xp, attn, lru.reshape(1, n_ctx * ctx_len, LRU_W),
                   four.reshape(1, n_ctx * ctx_len, FOURIER_W), mod, 0,
                   *vecs, w_out_l, w_gu_l, w_down_l, tm=POST_TILE)

        qt, k, vt, rest, xf = _inproj(
            xs, mod, 1, g_pre_mix[l][None, :], w_in_l, rope_tabs)
        attn = _attention(qt, k, vt, (ck, cv, l), w_lambda[l], g_subln[l], li,
                          n_seq=n_lat, seq_len=lat_len, n_heads=1, n_sub=8)
        lru, _ = _rg_lru(rest, conv_w[l], conv_b[l], w_gates, b_gates,
                         lru_lambda[l], state_lru[:, l])
        four = _fourier_fft(xf, fft_lat[0], fft_lat[1], fft_lat[2], fft_lat[3], chan_dft_stacked)
        xs = _post(xs, attn, lru, four, mod, 1,
                   *vecs, w_out_l, w_gu_l, w_down_l, tm=POST_TILE)

    return (xp.reshape(n_ctx, ctx_len, D_MODEL), xs,
            prev_k[-1].reshape(n_ctx, DEPTH, ctx_len, N_HEADS, 2, HEAD_DIM),
            prev_v[-1].reshape(n_ctx, DEPTH, ctx_len, N_HEADS, HEAD_W), jnp.stack(new_h, axis=1))
```

```python
import functools
import math

import jax
import jax.numpy as jnp
import numpy as np
from jax import lax
from jax.experimental import pallas as pl
from jax.experimental.pallas import tpu as pltpu

D_MODEL = 1024
DEPTH = 2
GRID_W = 64
HEAD_DIM = 64
N_HEADS = 4
HEAD_W = 2 * HEAD_DIM
ATTN_W = N_HEADS * HEAD_W
LRU_W = D_MODEL // 4
LRU_BLOCKS = 4
CONV_W = 4
FOURIER_W = D_MODEL // 4
FOURIER_GROUPS = 4
FOURIER_GW = FOURIER_W // FOURIER_GROUPS
MIX_W = ATTN_W + LRU_W + FOURIER_W
REST_W = 2 * LRU_W
IN_W = 3 * ATTN_W + REST_W + FOURIER_W
FFN_HIDDEN = -(-8 * D_MODEL // (3 * 256)) * 256
ROPE_BASE = 10000.0
RG_C = 8.0
EPS = 1e-6
N_MOD = 6
COND_ROWS = 8

F32 = jnp.float32
BF16 = jnp.bfloat16
V7X_VMEM_LIMIT_BYTES = 56 * 1024 * 1024
CONV_HALO = 8


def _cparams(*semantics, flags=None):
    return pltpu.CompilerParams(dimension_semantics=semantics,
                                vmem_limit_bytes=V7X_VMEM_LIMIT_BYTES, flags=flags)


def _resident(shape, index_map):
    return pl.BlockSpec(shape, index_map, pipeline_mode=pl.Buffered(1))


def _lambda_init(l):
    return 0.8 - 0.6 * math.exp(-0.3 * l)


def _rms(x, axis):
    return x * lax.rsqrt(jnp.mean(x * x, axis=axis, keepdims=True) + EPS)


def _mod_kernel(cond_ref, w_ref, b_ref, o_ref):
    s = cond_ref[...]
    s = s * jax.nn.sigmoid(s)
    o_ref[0] = jnp.dot(s, w_ref[0], precision=lax.Precision.HIGHEST,
                       preferred_element_type=F32) + b_ref[0]


def _modulation(cond, w_mod, b_mod):
    tn = 1536
    n_out = N_MOD * D_MODEL
    return pl.pallas_call(
        _mod_kernel,
        grid=(DEPTH, n_out // tn),
        in_specs=[pl.BlockSpec((COND_ROWS, D_MODEL), lambda l, j: (0, 0)),
                  pl.BlockSpec((1, D_MODEL, tn), lambda l, j: (l, 0, j)),
                  pl.BlockSpec((1, 1, tn), lambda l, j: (l, 0, j))],
        out_specs=pl.BlockSpec((1, COND_ROWS, tn), lambda l, j: (l, 0, j)),
        out_shape=jax.ShapeDtypeStruct((DEPTH, COND_ROWS, n_out), F32),
        compiler_params=_cparams("arbitrary", "arbitrary"),
        name="modulation",
    )(cond, w_mod, b_mod.reshape(DEPTH, 1, n_out))


INPROJ_TILE = 512
INPROJ_ROW_GROUP = 256


def _inproj_kernel(*refs, rope, kv_mode, n_prev):
    x_ref, mod_ref, g_ref, w_ref = refs[:4]
    refs = refs[4:]
    if rope:
        cos_ref, sin_lo_ref, sin_hi_ref = refs[:3]
        refs = refs[3:]
    prev_k, prev_v = refs[:n_prev], refs[n_prev:2 * n_prev]
    refs = refs[2 * n_prev:]
    qt_ref, k_ref, vt_ref, rest_ref, xf_ref = refs[:5]
    mod = mod_ref[0]
    tm = x_ref.shape[1]
    groups = [slice(r, r + INPROJ_ROW_GROUP) for r in range(0, tm, INPROJ_ROW_GROUP)]
    hs = [((_rms(x_ref[0, rows], -1) * g_ref[...]) * (1.0 + mod[1:2]) + mod[0:1]).astype(BF16)
          for rows in groups]
    projs = [jnp.dot(h, w_ref[...], preferred_element_type=F32) for h in hs]

    row = lax.broadcasted_iota(jnp.int32, (HEAD_W, INPROJ_ROW_GROUP), 0)
    scale = HEAD_DIM ** -0.5 * math.log2(math.e)
    for gi, (rows, proj) in enumerate(zip(groups, projs)):
        q = proj[:, :ATTN_W]
        k = proj[:, ATTN_W:2 * ATTN_W]
        v = proj[:, 2 * ATTN_W:3 * ATTN_W]
        if kv_mode == "flat":
            kf_ref, vf_ref = refs[5:7]
            kf_ref[rows] = k
            vf_ref[rows] = v
        elif kv_mode == "stacked":
            kf_ref, vf_ref = refs[5:7]
            for p in range(n_prev):
                kf_ref[gi, p] = prev_k[p][rows]
                vf_ref[gi, p] = prev_v[p][rows]
            kf_ref[gi, n_prev] = k
            vf_ref[gi, n_prev] = v

        def rotary(t):
            return (t * cos_ref[rows] + pltpu.roll(t, HEAD_W - 16, 1) * sin_lo_ref[rows]
                    + pltpu.roll(t, 16, 1) * sin_hi_ref[rows])

        for hd in range(N_HEADS):
            qh = q[:, hd * HEAD_W:(hd + 1) * HEAD_W]
            kh = k[:, hd * HEAD_W:(hd + 1) * HEAD_W]
            if rope:
                qh = rotary(qh)
                kh = rotary(kh)
            qht = (qh * scale).T
            qt_ref[0, 2 * hd, :, rows] = jnp.where(row < HEAD_DIM, qht, 0.0).astype(BF16)
            qt_ref[0, 2 * hd + 1, :, rows] = jnp.where(row >= HEAD_DIM, qht, 0.0).astype(BF16)
            k_ref[0, rows, hd * HEAD_W:(hd + 1) * HEAD_W] = kh.astype(BF16)
        vt_ref[0, :, rows] = v.T.astype(BF16)
        rest_ref[0, rows] = proj[:, 3 * ATTN_W:3 * ATTN_W + REST_W]
        xf_ref[0, rows] = proj[:, 3 * ATTN_W + REST_W:].astype(BF16)


def _inproj(x, mod, goff, g_pre, w_in, rope_tabs, kv_out=None):
    G, T, _ = x.shape
    tm = INPROJ_TILE
    rope = rope_tabs is not None
    in_specs = [pl.BlockSpec((1, tm, D_MODEL), lambda g, i: (g, i, 0)),
                pl.BlockSpec((1, N_MOD, D_MODEL), lambda g, i: (g + goff, 0, 0)),
                pl.BlockSpec((1, D_MODEL), lambda g, i: (0, 0)),
                _resident((D_MODEL, IN_W), lambda g, i: (0, 0))]
    args = [x, mod, g_pre, w_in]
    if rope:
        in_specs += [pl.BlockSpec((tm, HEAD_W), lambda g, i: (i, 0))] * 3
        args += list(rope_tabs)
    out_specs = [pl.BlockSpec((1, 2 * N_HEADS, HEAD_W, tm), lambda g, i: (g, 0, 0, i)),
                 pl.BlockSpec((1, tm, ATTN_W), lambda g, i: (g, i, 0)),
                 pl.BlockSpec((1, ATTN_W, tm), lambda g, i: (g, 0, i)),
                 pl.BlockSpec((1, tm, REST_W), lambda g, i: (g, i, 0)),
                 pl.BlockSpec((1, tm, FOURIER_W), lambda g, i: (g, i, 0))]
    out_shape = [jax.ShapeDtypeStruct((G, 2 * N_HEADS, HEAD_W, T), BF16),
                 jax.ShapeDtypeStruct((G, T, ATTN_W), BF16),
                 jax.ShapeDtypeStruct((G, ATTN_W, T), BF16),
                 jax.ShapeDtypeStruct((G, T, REST_W), F32),
                 jax.ShapeDtypeStruct((G, T, FOURIER_W), BF16)]
    kv_mode = None
    n_prev = 0
    if kv_out is not None:
        assert G == 1
        seq_len, last, prev_k, prev_v = kv_out
        if not last:
            kv_mode = "flat"
            out_specs += [pl.BlockSpec((tm, ATTN_W), lambda g, i: (i, 0))] * 2
            out_shape += [jax.ShapeDtypeStruct((T, ATTN_W), F32)] * 2
        else:
            assert seq_len == INPROJ_ROW_GROUP
            kv_mode = "stacked"
            n_prev = len(prev_k)
            in_specs += [pl.BlockSpec((tm, ATTN_W), lambda g, i: (i, 0))] * (2 * n_prev)
            args += list(prev_k) + list(prev_v)
            out_specs += [pl.BlockSpec((tm // seq_len, n_prev + 1, seq_len, ATTN_W),
                                       lambda g, i: (i, 0, 0, 0))] * 2
            out_shape += [jax.ShapeDtypeStruct((T // seq_len, n_prev + 1, seq_len, ATTN_W), F32)] * 2
    return pl.pallas_call(
        functools.partial(_inproj_kernel, rope=rope, kv_mode=kv_mode, n_prev=n_prev),
        grid=(G, T // tm),
        in_specs=in_specs, out_specs=out_specs, out_shape=out_shape,
        compiler_params=_cparams("arbitrary", "arbitrary"),
        name="inproj_rope" if rope else "inproj_ctx",
    )(*args)


QUERY_BLOCK = 256
KEY_CHUNK = 256
SUBLANES = 8
SUM_ACCUMULATORS = 2


ROW_ACCUMULATORS = 2


def _accumulate_rows(accs, x, op):
    for idx, r in enumerate(range(0, x.shape[0], SUBLANES)):
        j = idx % len(accs)
        part = x[r:r + SUBLANES]
        accs[j] = part if accs[j] is None else op(accs[j], part)


def _finish_rows(accs, op, reduce):
    total = accs[0]
    for a in accs[1:]:
        total = op(total, a)
    return reduce(total, axis=0, keepdims=True)


def _attn_kernel(*refs, has_cache, li, n_heads, n_sub, n_keys):
    qt_ref, k_ref, vt_ref = refs[:3]
    refs = refs[3:]
    if has_cache:
        ck_ref, cv_ref = refs[:2]
        refs = refs[2:]
    wl_ref, g_ref, o_ref = refs[:3]
    s_bufs = refs[3:5]
    acc_bufs = refs[5:7]
    n_self = k_ref.shape[1]
    if has_cache:
        kall, vtall = refs[7:9]

        @pl.when(pl.program_id(2) == 0)
        def _():
            kall[0:n_self] = k_ref[0]
            kall[n_self:n_keys] = ck_ref[0].astype(BF16)
            vtall[:, 0:n_self] = vt_ref[0]
            vtall[:, n_self:n_keys] = cv_ref[0].T.astype(BF16)

    def k_chunk(hd, c):
        rows = slice(c * KEY_CHUNK, (c + 1) * KEY_CHUNK)
        return kall[rows, :] if has_cache else k_ref[0, rows, hd * HEAD_W:(hd + 1) * HEAD_W]

    def vt_chunk(hd, c):
        cols = slice(c * KEY_CHUNK, (c + 1) * KEY_CHUNK)
        return vtall[:, cols] if has_cache else vt_ref[0, hd * HEAD_W:(hd + 1) * HEAD_W, cols]

    units = [(hd, sub, m) for hd in range(n_heads) for sub in range(n_sub) for m in range(2)]
    n_chunks = n_keys // KEY_CHUNK
    results = {}
    col_max = None
    for t in range(len(units) + 1):
        if t < len(units):
            hd, sub, m = units[t]
            qt = qt_ref[0, 2 * hd + m, :, sub * QUERY_BLOCK:(sub + 1) * QUERY_BLOCK]
            new_max = [None] * ROW_ACCUMULATORS
        if t > 0:
            hd_p = units[t - 1][0]
            mx = _finish_rows(col_max, jnp.maximum, jnp.max)
            den = [None] * SUM_ACCUMULATORS
            acc_ref = acc_bufs[(t - 1) % 2]
        for c in range(n_chunks):
            rows = slice(c * KEY_CHUNK, (c + 1) * KEY_CHUNK)
            if t < len(units):
                s = jnp.dot(k_chunk(hd, c), qt, preferred_element_type=F32)
                s_bufs[t % 2][rows, :] = s
                _accumulate_rows(new_max, s, jnp.maximum)
            if t > 0:
                e = jnp.exp2(s_bufs[(t - 1) % 2][rows, :] - mx)
                _accumulate_rows(den, e, jnp.add)
                pv = jnp.dot(vt_chunk(hd_p, c), e.astype(BF16), preferred_element_type=F32)
                if c == 0:
                    acc_ref[...] = pv
                else:
                    acc_ref[...] += pv
        if t > 0:
            results[units[t - 1]] = acc_ref[...] / _finish_rows(den, jnp.add, jnp.sum)
        col_max = new_max

    wl = wl_ref[...]
    lam = (jnp.exp(jnp.sum(wl[0:1] * wl[1:2], axis=-1, keepdims=True))
           - jnp.exp(jnp.sum(wl[2:3] * wl[3:4], axis=-1, keepdims=True)) + li)
    for hd in range(n_heads):
        for sub in range(n_sub):
            ot = results[(hd, sub, 0)] - lam * results[(hd, sub, 1)]
            y = (_rms(ot, 0) * g_ref[...]) * (1.0 - li)
            o_ref[0, sub * QUERY_BLOCK:(sub + 1) * QUERY_BLOCK,
                  hd * HEAD_W:(hd + 1) * HEAD_W] = y.T.astype(BF16)


def _attention(qt, k, vt, cache, w_lambda_l, g_subln_l, li, n_seq, seq_len, n_heads, n_sub):
    G, _, _, T = qt.shape
    per_g = T // seq_len
    tq = n_sub * QUERY_BLOCK
    nq = seq_len // tq
    has_cache = cache is not None
    n_keys = seq_len

    def gi(b):
        return b // per_g

    def si(b):
        return b % per_g

    hw = n_heads * HEAD_W
    in_specs = [pl.BlockSpec((1, 2 * n_heads, HEAD_W, tq), lambda b, h, i: (gi(b), h, 0, si(b) * nq + i)),
                pl.BlockSpec((1, seq_len, hw), lambda b, h, i: (gi(b), si(b), h)),
                pl.BlockSpec((1, hw, seq_len), lambda b, h, i: (gi(b), h, si(b)))]
    args = [qt, k, vt]
    if has_cache:
        assert n_heads == 1
        ck, cv, layer = cache
        past = ck.shape[2]
        n_keys = seq_len + past
        in_specs += [pl.BlockSpec((1, None, past, HEAD_W), lambda b, h, i: (b, layer, 0, h))] * 2
        args += [ck, cv]
    scratch = [pltpu.VMEM((n_keys, QUERY_BLOCK), F32)] * 2
    scratch += [pltpu.VMEM((HEAD_W, QUERY_BLOCK), F32)] * 2
    if has_cache:
        scratch += [pltpu.VMEM((n_keys, HEAD_W), BF16), pltpu.VMEM((HEAD_W, n_keys), BF16)]
    in_specs += [pl.BlockSpec((4, HEAD_DIM), lambda b, h, i: (0, 0)),
                 pl.BlockSpec((HEAD_W, 1), lambda b, h, i: (0, 0))]
    args += [w_lambda_l, g_subln_l.reshape(HEAD_W, 1)]
    return pl.pallas_call(
        functools.partial(_attn_kernel, has_cache=has_cache, li=li, n_heads=n_heads,
                          n_sub=n_sub, n_keys=n_keys),
        grid=(n_seq, N_HEADS // n_heads, nq),
        in_specs=in_specs,
        out_specs=pl.BlockSpec((1, tq, hw), lambda b, h, i: (gi(b), si(b) * nq + i, h)),
        out_shape=jax.ShapeDtypeStruct((G, T, ATTN_W), BF16),
        scratch_shapes=scratch,
        compiler_params=_cparams("arbitrary", "arbitrary", "arbitrary"),
        name="attn_latent" if has_cache else "attn_ctx",
    )(*args)


def _scan_chunk(a, b, carry, reverse):
    n, w = a.shape
    groups = n // SUBLANES
    a = a.reshape(groups, SUBLANES, w)
    b = b.reshape(groups, SUBLANES, w)
    sub = lax.broadcasted_iota(jnp.int32, (groups, SUBLANES, w), 1)
    d = 1
    while d < SUBLANES:
        if reverse:
            shift, valid = SUBLANES - d, sub < SUBLANES - d
        else:
            shift, valid = d, sub >= d
        a_prev = jnp.where(valid, pltpu.roll(a, shift, 1), 1.0)
        b_prev = jnp.where(valid, pltpu.roll(b, shift, 1), 0.0)
        b = a * b_prev + b
        a = a * a_prev
        d *= 2
    edge = 0 if reverse else SUBLANES - 1
    hs = [None] * groups
    for g in (range(groups - 1, -1, -1) if reverse else range(groups)):
        hs[g] = a[g] * carry + b[g]
        carry = hs[g][edge:edge + 1]
    return jnp.concatenate(hs, axis=0), carry


def _lru_kernel(xr_ref, gr_ref, cw_ref, cb_ref, w_ref, bias_ref, lam_ref, h0_ref,
                out_ref, hfin_ref, xpad, hf_scr, ab_scr, bb_scr, *, seq_len, tc):
    nch = seq_len // tc
    zeros_halo = jnp.zeros((CONV_HALO, LRU_W), F32)
    xpad[0:CONV_HALO] = zeros_halo
    xpad[seq_len + CONV_HALO:seq_len + 2 * CONV_HALO] = zeros_halo
    xpad[CONV_HALO:seq_len + CONV_HALO] = xr_ref[0]
    lam_p = lam_ref[...]
    neg = -lam_p
    softplus = jnp.maximum(neg, 0.0) + jnp.log1p(jnp.exp(-jnp.abs(neg)))
    coef = -RG_C * softplus
    cw = cw_ref[...]
    h0 = h0_ref[0]

    def decay_and_input(xc, z, direction):
        off = direction * 2 * LRU_W
        r = jax.nn.sigmoid(z[:, off:off + LRU_W])
        i = jax.nn.sigmoid(z[:, off + LRU_W:off + 2 * LRU_W])
        log_a = coef[direction:direction + 1] * r
        a = jnp.exp(log_a)
        b = jnp.sqrt(-jnp.tanh(log_a) * (1.0 + a * a)) * (i * xc)
        return a, b

    def forward_chunk(c, carry):
        start = pl.multiple_of(c * tc, tc)
        win = xpad[pl.ds(start, tc + 2 * CONV_HALO), :]
        lo = CONV_HALO - CONV_W // 2
        xc = win[lo:lo + tc] * cw[0:1]
        for j in range(1, CONV_W):
            xc = xc + win[lo + j:lo + j + tc] * cw[j:j + 1]
        xc = xc + cb_ref[...]
        z = jnp.dot(xc.astype(BF16), w_ref[...], preferred_element_type=F32) + bias_ref[...]
        a_f, b_f = decay_and_input(xc, z, 0)
        hf, carry = _scan_chunk(a_f, b_f, carry, reverse=False)
        hf_scr[pl.ds(start, tc), :] = hf
        a_b, b_b = decay_and_input(xc, z, 1)
        ab_scr[pl.ds(start, tc), :] = a_b
        bb_scr[pl.ds(start, tc), :] = b_b
        return carry

    def backward_chunk(j, carry):
        c = nch - 1 - j
        start = pl.multiple_of(c * tc, tc)
        hb, carry = _scan_chunk(ab_scr[pl.ds(start, tc), :], bb_scr[pl.ds(start, tc), :],
                                carry, reverse=True)
        gate = jax.nn.gelu(gr_ref[0, pl.ds(start, tc), :], approximate=True)
        out_ref[0, pl.ds(start, tc), :] = ((hf_scr[pl.ds(start, tc), :] + hb) * gate).astype(BF16)
        return carry

    if nch == 1:
        hf_fin = forward_chunk(0, h0[0:1])
        hb_fin = backward_chunk(0, h0[1:2])
    else:
        hf_fin = lax.fori_loop(0, nch, forward_chunk, h0[0:1])
        hb_fin = lax.fori_loop(0, nch, backward_chunk, h0[1:2])
    hfin_ref[0, 0:1] = hf_fin
    hfin_ref[0, 1:2] = hb_fin


def _rg_lru(rest, conv_w, conv_b, w_gates, b_gates, lam_p, h0):
    B, L, _ = rest.shape
    tc = min(L, 256)
    return pl.pallas_call(
        functools.partial(_lru_kernel, seq_len=L, tc=tc),
        grid=(B,),
        in_specs=[pl.BlockSpec((1, L, LRU_W), lambda b: (b, 0, 0)),
                  pl.BlockSpec((1, L, LRU_W), lambda b: (b, 0, 1)),
                  pl.BlockSpec((CONV_W, LRU_W), lambda b: (0, 0)),
                  pl.BlockSpec((1, LRU_W), lambda b: (0, 0)),
                  pl.BlockSpec((LRU_W, 4 * LRU_W), lambda b: (0, 0)),
                  pl.BlockSpec((1, 4 * LRU_W), lambda b: (0, 0)),
                  pl.BlockSpec((2, LRU_W), lambda b: (0, 0)),
                  pl.BlockSpec((1, 2, LRU_W), lambda b: (b, 0, 0))],
        out_specs=[pl.BlockSpec((1, L, LRU_W), lambda b: (b, 0, 0)),
                   pl.BlockSpec((1, 2, LRU_W), lambda b: (b, 0, 0))],
        out_shape=[jax.ShapeDtypeStruct((B, L, LRU_W), BF16),
                   jax.ShapeDtypeStruct((B, 2, LRU_W), F32)],
        scratch_shapes=[pltpu.VMEM((L + 2 * CONV_HALO, LRU_W), F32),
                        pltpu.VMEM((L, LRU_W), F32),
                        pltpu.VMEM((L, LRU_W), F32),
                        pltpu.VMEM((L, LRU_W), F32)],
        compiler_params=_cparams("arbitrary"),
        name="rg_lru",
    )(rest, rest, conv_w, conv_b.reshape(1, LRU_W), w_gates, b_gates, lam_p, h0)


def _dft_direct_kernel(x_ref, w_ref, fc_ref, fs_ref, o_ref, *, scale):
    y = jnp.dot(x_ref[0], w_ref[...], preferred_element_type=F32)
    yc = y[:, :FOURIER_W].astype(BF16)
    ys = y[:, FOURIER_W:].astype(BF16)
    acc = jnp.dot(fc_ref[...], yc, preferred_element_type=F32)
    acc = acc - jnp.dot(fs_ref[...], ys, preferred_element_type=F32)
    o_ref[...] = (acc * scale).astype(BF16)


def _fourier_direct(xf, chan_dft, seq_cos, seq_sin):
    B, L, _ = xf.shape
    return pl.pallas_call(
        functools.partial(_dft_direct_kernel, scale=(L * FOURIER_GW) ** -0.5),
        grid=(B,),
        in_specs=[pl.BlockSpec((1, L, FOURIER_W), lambda b: (b, 0, 0)),
                  pl.BlockSpec((FOURIER_W, 2 * FOURIER_W), lambda b: (0, 0)),
                  pl.BlockSpec((L, L), lambda b: (0, 0)),
                  pl.BlockSpec((L, L), lambda b: (0, 0))],
        out_specs=pl.BlockSpec((L, FOURIER_W), lambda b: (b, 0)),
        out_shape=jax.ShapeDtypeStruct((B * L, FOURIER_W), BF16),
        compiler_params=_cparams("arbitrary"),
        name="fourier_direct",
    )(xf, chan_dft, seq_cos, seq_sin)


FFT_RADIX = 64
FFT_COLS = 2048
FFT_K1_PER_STEP = 8


def _fft_stage1_kernel(x_ref, f_ref, tc_ref, ts_ref, o_ref):
    a = jnp.dot(f_ref[...], x_ref[0], preferred_element_type=F32)
    a_re, a_im = a[:FFT_RADIX], a[FFT_RADIX:]
    tc, ts = tc_ref[...], ts_ref[...]
    o_ref[0, 0] = (a_re * tc + a_im * ts).astype(BF16)
    o_ref[0, 1] = (a_im * tc - a_re * ts).astype(BF16)


def _fft_stage2_kernel(b_ref, f_ref, w_ref, p_ref, o_ref, *, scale):
    zs = []
    for j in range(FFT_K1_PER_STEP):
        bj = jnp.concatenate([b_ref[0, 0, j], b_ref[0, 1, j]], axis=0)
        z = jnp.dot(f_ref[...], bj, preferred_element_type=F32)
        zs.append(jnp.concatenate([z[:FFT_RADIX], z[FFT_RADIX:]], axis=1))
    z_all = jnp.concatenate(zs, axis=0).astype(BF16)
    out = (jnp.dot(z_all, w_ref[...], preferred_element_type=F32) * scale).astype(BF16)
    out = jnp.dot(p_ref[...], out, preferred_element_type=F32)
    o_ref[0] = out.reshape(FFT_RADIX, FFT_K1_PER_STEP, FOURIER_W).astype(BF16)


def _fourier_fft(xf, stage1_dft, twiddle_cos, twiddle_sin, stage2_dft, chan_dft_stacked):
    B, L, _ = xf.shape
    R = FFT_RADIX
    assert L == R * R
    cols = R * FOURIER_W
    x2 = xf.reshape(B, R, cols)
    mid = pl.pallas_call(
        _fft_stage1_kernel,
        grid=(cols // FFT_COLS, B),
        in_specs=[pl.BlockSpec((1, R, FFT_COLS), lambda j, b: (b, 0, j)),
                  pl.BlockSpec((2 * R, R), lambda j, b: (0, 0)),
                  pl.BlockSpec((R, FFT_COLS), lambda j, b: (0, j)),
                  pl.BlockSpec((R, FFT_COLS), lambda j, b: (0, j))],
        out_specs=pl.BlockSpec((1, 2, R, FFT_COLS), lambda j, b: (b, 0, 0, j)),
        out_shape=jax.ShapeDtypeStruct((B, 2, R, cols), BF16),
        compiler_params=_cparams("arbitrary", "arbitrary"),
        name="fft_stage1",
    )(x2, stage1_dft, twiddle_cos, twiddle_sin)
    mid = mid.reshape(B, 2, R, R, FOURIER_W)
    kb = FFT_K1_PER_STEP
    src = np.arange(kb * R).reshape(kb, R).T.reshape(-1)
    perm = jnp.asarray(np.eye(kb * R)[src], dtype=F32).astype(BF16)
    out = pl.pallas_call(
        functools.partial(_fft_stage2_kernel, scale=(L * FOURIER_GW) ** -0.5),
        grid=(B, R // kb),
        in_specs=[pl.BlockSpec((1, 2, kb, R, FOURIER_W), lambda b, i: (b, 0, i, 0, 0)),
                  pl.BlockSpec((2 * R, 2 * R), lambda b, i: (0, 0)),
                  pl.BlockSpec((2 * FOURIER_W, FOURIER_W), lambda b, i: (0, 0)),
                  pl.BlockSpec((kb * R, kb * R), lambda b, i: (0, 0))],
        out_specs=pl.BlockSpec((1, R, kb, FOURIER_W), lambda b, i: (b, 0, i, 0)),
        out_shape=jax.ShapeDtypeStruct((B, R, R, FOURIER_W), BF16),
        compiler_params=_cparams("arbitrary", "arbitrary"),
        name="fft_stage2",
    )(mid, stage2_dft, chan_dft_stacked, perm)
    return out.reshape(B, L, FOURIER_W)


POST_TILE = 512
POST_ROW_GROUP = 256
def _post_kernel(x_ref, attn_ref, lru_ref, four_ref, mod_ref, gmix_ref, gpre_ref, gffn_ref,
                 wo_ref, wgu_ref, wd_ref, o_ref):
    mod = mod_ref[0]
    tm = x_ref.shape[1]
    groups = [slice(r, r + POST_ROW_GROUP) for r in range(0, tm, POST_ROW_GROUP)]

    def mix_out(rows):
        o = jnp.dot(attn_ref[0, rows], wo_ref[0:ATTN_W], preferred_element_type=F32)
        o = o + jnp.dot(lru_ref[0, rows], wo_ref[ATTN_W:ATTN_W + LRU_W], preferred_element_type=F32)
        return o + jnp.dot(four_ref[0, rows], wo_ref[ATTN_W + LRU_W:MIX_W], preferred_element_type=F32)

    def residual_and_prenorm(rows, o):
        x1 = x_ref[0, rows] + mod[2:3] * (_rms(o, -1) * gmix_ref[...])
        h = (_rms(x1, -1) * gpre_ref[...]) * (1.0 + mod[4:5]) + mod[3:4]
        return x1, h.astype(BF16)

    def gate_up(hb):
        gt = jnp.dot(hb, wgu_ref[:, :FFN_HIDDEN], preferred_element_type=F32)
        up = jnp.dot(hb, wgu_ref[:, FFN_HIDDEN:], preferred_element_type=F32)
        return ((gt * jax.nn.sigmoid(gt)) * up).astype(BF16)

    mixed = [mix_out(rows) for rows in groups]
    normed = [residual_and_prenorm(rows, o) for rows, o in zip(groups, mixed)]
    acts = [gate_up(hb) for _, hb in normed]
    downs = [jnp.dot(act, wd_ref[...], preferred_element_type=F32) for act in acts]
    for rows, (x1, _), f in zip(groups, normed, downs):
        o_ref[0, rows] = x1 + mod[5:6] * (_rms(f, -1) * gffn_ref[...])


def _post(x, attn, lru, four, mod, goff, g_post_mix, g_pre_ffn, g_post_ffn,
          w_out, w_gate_up, w_down, tm):
    G, T, _ = x.shape
    vec = pl.BlockSpec((1, D_MODEL), lambda g, i: (0, 0))
    return pl.pallas_call(
        _post_kernel,
        grid=(G, T // tm),
        in_specs=[pl.BlockSpec((1, tm, D_MODEL), lambda g, i: (g, i, 0)),
                  pl.BlockSpec((1, tm, ATTN_W), lambda g, i: (g, i, 0)),
                  pl.BlockSpec((1, tm, LRU_W), lambda g, i: (g, i, 0)),
                  pl.BlockSpec((1, tm, FOURIER_W), lambda g, i: (g, i, 0)),
                  pl.BlockSpec((1, N_MOD, D_MODEL), lambda g, i: (g + goff, 0, 0)),
                  vec, vec, vec,
                  _resident((MIX_W, D_MODEL), lambda g, i: (0, 0)),
                  _resident((D_MODEL, 2 * FFN_HIDDEN), lambda g, i: (0, 0)),
                  _resident((FFN_HIDDEN, D_MODEL), lambda g, i: (0, 0))],
        out_specs=pl.BlockSpec((1, tm, D_MODEL), lambda g, i: (g, i, 0)),
        out_shape=jax.ShapeDtypeStruct((G, T, D_MODEL), F32),
        compiler_params=_cparams("arbitrary", "arbitrary"),
        name="post",
    )(x, attn, lru, four, mod, g_post_mix, g_pre_ffn, g_post_ffn, w_out, w_gate_up, w_down)


def _rope_tables(n_tokens):
    rows = n_tokens // GRID_W
    row = jnp.repeat(jnp.arange(rows, dtype=F32), GRID_W)
    col = jnp.tile(jnp.arange(GRID_W, dtype=F32), rows)
    n = HEAD_DIM // 4
    inv = ROPE_BASE ** (-jnp.arange(n, dtype=F32) / n)
    ang_r, ang_c = row[:, None] * inv, col[:, None] * inv
    zero = jnp.zeros_like(ang_r)
    cos64 = jnp.concatenate([jnp.cos(ang_r)] * 2 + [jnp.cos(ang_c)] * 2, axis=-1)
    sin_lo = jnp.concatenate([-jnp.sin(ang_r), zero, -jnp.sin(ang_c), zero], axis=-1)
    sin_hi = jnp.concatenate([zero, jnp.sin(ang_r), zero, jnp.sin(ang_c)], axis=-1)
    return tuple(jnp.tile(t, (1, 2)) for t in (cos64, sin_lo, sin_hi))


def _seq_dft_tables(n):
    r = 1 << (n.bit_length() // 2)
    l = jnp.arange(n, dtype=jnp.int32)

    def cos_sin(rows, period):
        k = jnp.arange(rows, dtype=jnp.int32)
        ang = ((k[:, None] * l[None, :]) % period).astype(F32) * (2.0 * math.pi / period)
        return jnp.cos(ang), jnp.sin(ang)

    ca, sa = cos_sin(n // r, n // r)
    cb, sb = cos_sin(r, n)
    cos = ca[:, None, :] * cb[None] - sa[:, None, :] * sb[None]
    sin = sa[:, None, :] * cb[None] + ca[:, None, :] * sb[None]
    return cos.reshape(n, n).astype(BF16), sin.reshape(n, n).astype(BF16)


def _chan_dft_table():
    k = np.arange(FOURIER_GW)
    ang = 2.0 * np.pi * ((k[:, None] * k[None, :]) % FOURIER_GW) / FOURIER_GW
    eye = np.eye(FOURIER_GROUPS)
    table = np.concatenate([np.kron(eye, np.cos(ang)), np.kron(eye, np.sin(ang))], axis=1)
    return jnp.asarray(table, dtype=F32).astype(BF16)


def _fft_tables(n):
    r = FFT_RADIX
    k = np.arange(r)
    ang = 2.0 * np.pi * ((k[:, None] * k[None, :]) % r) / r
    cm, sm = np.cos(ang), np.sin(ang)
    stage1 = np.concatenate([cm, -sm], axis=0)
    stage2 = np.block([[cm, sm], [-sm, cm]])
    kj = jnp.arange(r, dtype=jnp.int32)
    tw = (kj[:, None] * kj[None, :]).astype(F32) * (2.0 * math.pi / n)
    expand = lambda t: jnp.repeat(t, FOURIER_W, axis=1)
    return (jnp.asarray(stage1, dtype=F32).astype(BF16), expand(jnp.cos(tw)), expand(jnp.sin(tw)),
            jnp.asarray(stage2, dtype=F32).astype(BF16))


def _block_diag(w):
    n, bw, _ = w.shape
    eye = jnp.eye(n, dtype=w.dtype)
    return (eye[:, None, :, None] * w[:, :, None, :]).reshape(n * bw, n * bw)


def kernel(x_prompt, x_sample, cache_k, cache_v, state_lru, c, c_ctx, w_mod, b_mod, g_pre_mix, g_post_mix, g_pre_ffn, g_post_ffn, w_in, w_out, w_lambda, g_subln, conv_w, conv_b, lru_wa, lru_ba, lru_wx, lru_bx, lru_lambda, w_gate_up, w_down):
    n_ctx, ctx_len, _ = x_prompt.shape
    n_lat, lat_len, _ = x_sample.shape
    past = cache_k.shape[2]

    cond = jnp.concatenate(
        [c_ctx[None, :], c, jnp.zeros((COND_ROWS - 1 - n_lat, D_MODEL), F32)], axis=0)
    mod_all = _modulation(cond, w_mod, b_mod).reshape(DEPTH, COND_ROWS, N_MOD, D_MODEL)

    rope_tabs = _rope_tables(lat_len)
    chan_dft = _chan_dft_table()
    chan_dft_stacked = jnp.concatenate([chan_dft[:, :FOURIER_W], chan_dft[:, FOURIER_W:]], axis=0)
    dft_ctx = _seq_dft_tables(ctx_len)
    fft_lat = _fft_tables(lat_len)
    ck = cache_k.reshape(n_lat, DEPTH, past, ATTN_W)
    cv = cache_v.reshape(n_lat, DEPTH, past, ATTN_W)
    h0_ctx = jnp.zeros((n_ctx, 2, LRU_W), F32)

    xp = x_prompt.reshape(1, n_ctx * ctx_len, D_MODEL)
    xs = x_sample
    prev_k, prev_v, new_h = [], [], []
    for l in range(DEPTH):
        li = _lambda_init(l)
        mod = mod_all[l]
        w_in_l = w_in[l].astype(BF16)
        w_out_l = w_out[l].astype(BF16)
        w_gu_l = w_gate_up[l].astype(BF16)
        w_down_l = w_down[l].astype(BF16)
        w_gates = jnp.concatenate(
            [_block_diag(lru_wa[l, 0]), _block_diag(lru_wx[l, 0]),
             _block_diag(lru_wa[l, 1]), _block_diag(lru_wx[l, 1])], axis=1).astype(BF16)
        b_gates = jnp.concatenate(
            [lru_ba[l, 0], lru_bx[l, 0], lru_ba[l, 1], lru_bx[l, 1]])[None, :]
        vecs = [g[l][None, :] for g in (g_post_mix, g_pre_ffn, g_post_ffn)]

        qt, k, vt, rest, xf, k_f32, v_f32 = _inproj(
            xp, mod, 0, g_pre_mix[l][None, :], w_in_l, None,
            kv_out=(ctx_len, l == DEPTH - 1, prev_k, prev_v))
        prev_k.append(k_f32)
        prev_v.append(v_f32)
        attn = _attention(qt, k, vt, None, w_lambda[l], g_subln[l], li,
                          n_seq=n_ctx, seq_len=ctx_len, n_heads=N_HEADS,
                          n_sub=ctx_len // QUERY_BLOCK)
        rest_seq = rest.reshape(n_ctx, ctx_len, REST_W)
        lru, h_fin = _rg_lru(rest_seq, conv_w[l], conv_b[l], w_gates, b_gates,
                             lru_lambda[l], h0_ctx)
        new_h.append(h_fin)
        four = _fourier_direct(xf.reshape(n_ctx, ctx_len, FOURIER_W), chan_dft, *dft_ctx)
        xp = _post(xp, attn, lru.reshape(1, n_ctx * ctx_len, LRU_W),
                   four.reshape(1, n_ctx * ctx_len, FOURIER_W), mod, 0,
                   *vecs, w_out_l, w_gu_l, w_down_l, tm=POST_TILE)

        qt, k, vt, rest, xf = _inproj(
            xs, mod, 1, g_pre_mix[l][None, :], w_in_l, rope_tabs)
        attn = _attention(qt, k, vt, (ck, cv, l), w_lambda[l], g_subln[l], li,
                          n_seq=n_lat, seq_len=lat_len, n_heads=1, n_sub=16)
        lru, _ = _rg_lru(rest, conv_w[l], conv_b[l], w_gates, b_gates,
                         lru_lambda[l], state_lru[:, l])
        four = _fourier_fft(xf, fft_lat[0], fft_lat[1], fft_lat[2], fft_lat[3], chan_dft_stacked)
        xs = _post(xs, attn, lru, four, mod, 1,
                   *vecs, w_out_l, w_gu_l, w_down_l, tm=POST_TILE)

    return (xp.reshape(n_ctx, ctx_len, D_MODEL), xs,
            prev_k[-1].reshape(n_ctx, DEPTH, ctx_len, N_HEADS, 2, HEAD_DIM),
            prev_v[-1].reshape(n_ctx, DEPTH, ctx_len, N_HEADS, HEAD_W), jnp.stack(new_h, axis=1))
```
